```python
import math
import jax, jax.numpy as jnp
from jax import lax
import numpy as np

D_MODEL = 2048
BATCH = 2
SEQ = 4096
DEPTH = 4
DEC_BATCH = 8
DEC_SEQ = 8
PAST_LEN = 16384
PAGE_SIZE = 128

MIX_W = D_MODEL
RW_HD = 64
RW_W = MIX_W // 4
RW_HEADS = RW_W // RW_HD
RW_DECAY_LORA = 32
RW_A_LORA = 32
RW_G_LORA = 96
RW_GN_EPS = 64e-5
RW_COLS = 3 * RW_W + RW_DECAY_LORA + RW_A_LORA + RW_G_LORA
RW_SPLITS = (RW_W, 2 * RW_W, 3 * RW_W, 3 * RW_W + RW_DECAY_LORA, 3 * RW_W + RW_DECAY_LORA + RW_A_LORA)
FX_HD = 128
FX_W = MIX_W // 2
FX_HEADS = FX_W // FX_HD
FX_COLS = 3 * FX_W + FX_HEADS
FX_SPLITS = (FX_W, 2 * FX_W, 3 * FX_W)
Q_BLOCK = 128
GL_VW = MIX_W - RW_W - FX_W
GL_HEADS = 4
GL_DV = GL_VW // GL_HEADS
GL_DK = GL_DV // 2
GL_KW = GL_HEADS * GL_DK
GL_GATE_LORA = 16
GL_TAU = 16.0
GL_CHUNK = 16
GL_COLS = 2 * GL_KW + GL_VW + GL_GATE_LORA + GL_VW
GL_SPLITS = (GL_KW, 2 * GL_KW, 2 * GL_KW + GL_VW, 2 * GL_KW + GL_VW + GL_GATE_LORA)

IN_COLS = RW_COLS + FX_COLS + GL_COLS
D_FF = -(-8 * D_MODEL // (3 * 256)) * 256
NORM_EPS = 1e-6

kernel_name = "hymba_rwkv7_fox_gla_decode_step"


def _rms(x, g):
    xf = x.astype(jnp.float32)
    y = xf * lax.rsqrt(jnp.mean(xf * xf, axis=-1, keepdims=True) + NORM_EPS)
    return (y * g.astype(jnp.float32)).astype(x.dtype)


def _heads(t, h):
    return t.reshape(t.shape[:-1] + (h, t.shape[-1] // h))


def _gather_pages(pool, page_table, l):
    g = pool[page_table, l]
    return g.reshape((g.shape[0], g.shape[1] * g.shape[2]) + g.shape[3:])


def _rwkv7(p, prev_row, s0, mu, w_b, w0, a_b, a0, g_b, k_k, k_a, r_k, lnx_g, lnx_b):
    B, T, _ = p.shape
    f32 = jnp.float32
    p_prev = jnp.concatenate([prev_row[:, None, :].astype(p.dtype), p[:, :-1]], axis=1)
    ps = p + mu * (p_prev - p)
    r, k, v, xw, xa, xg = jnp.split(ps, RW_SPLITS, axis=-1)
    log_w = -jnp.exp(-jax.nn.softplus(-(w0 + jnp.tanh(xw) @ w_b)) - 0.5)
    a = jax.nn.sigmoid(a0 + xa @ a_b)
    g = jax.nn.sigmoid(xg) @ g_b
    kk = _heads(k * k_k, RW_HEADS).astype(f32)
    kk = kk / jnp.maximum(jnp.sqrt(jnp.sum(kk * kk, axis=-1, keepdims=True)), 1e-12)
    k = k * (1.0 + (a - 1.0) * k_a)
    r_h, k_h, v_h, a_h, lw_h = [_heads(t, RW_HEADS).astype(f32) for t in (r, k, v, a, log_w)]

    def step(S, xs):
        rt, kt, vt, at, lwt, kkt = xs
        s_kk = jnp.einsum('bhvk,bhk->bhv', S, kkt)
        S = (S * jnp.exp(lwt)[:, :, None, :]
             - s_kk[..., None] * (at * kkt)[:, :, None, :]
             + vt[..., None] * kt[:, :, None, :])
        return S, jnp.einsum('bhvk,bhk->bhv', S, rt)

    xs = tuple(jnp.moveaxis(t, 1, 0) for t in (r_h, k_h, v_h, a_h, lw_h, kk))
    S, y = lax.scan(step, s0.astype(f32), xs)
    y = jnp.moveaxis(y, 0, 1)
    mean = jnp.mean(y, axis=-1, keepdims=True)
    var = jnp.mean(jnp.square(y - mean), axis=-1, keepdims=True)
    y = ((y - mean) * lax.rsqrt(var + RW_GN_EPS)).reshape(B, T, RW_W) * lnx_g + lnx_b
    bonus = (jnp.sum(r_h * k_h * r_k, axis=-1, keepdims=True) * v_h).reshape(B, T, RW_W)
    out = (y + bonus) * g
    return out.astype(p.dtype), p[:, -1], S.astype(s0.dtype)


def _fox_project(p, qn_g, kn_g, b_f):
    q, k, v, fl = jnp.split(p, FX_SPLITS, axis=-1)
    q = _rms(_heads(q, FX_HEADS), qn_g) * FX_HD ** -0.5
    k = _rms(_heads(k, FX_HEADS), kn_g)
    v = _heads(v, FX_HEADS)
    logf = jax.nn.log_sigmoid((fl + b_f).astype(jnp.float32))
    return q, k, v, logf


def _fox_attend(q, k, v, logf, q_off):
    B, Tq, H, _ = q.shape
    L = k.shape[1]
    blk = math.gcd(Tq, Q_BLOCK)
    c = jnp.cumsum(logf.astype(jnp.float32), axis=1).transpose(0, 2, 1)
    c_q = c[:, :, q_off:q_off + Tq]
    kpos = jnp.arange(L)

    def block(i):
        start = i * blk
        qb = lax.dynamic_slice_in_dim(q, start, blk, axis=1)
        cb = lax.dynamic_slice_in_dim(c_q, start, blk, axis=2)
        qpos = q_off + start + jnp.arange(blk)
        s = jnp.einsum('bqhd,bkhd->bhqk', qb, k, preferred_element_type=jnp.float32)
        s = s + cb[..., None] - c[:, :, None, :]
        s = jnp.where(kpos[None, :] <= qpos[:, None], s, -jnp.inf)
        w = jax.nn.softmax(s, axis=-1)
        return jnp.einsum('bhqk,bkhd->bqhd', w.astype(v.dtype), v, preferred_element_type=jnp.float32)

    o = lax.map(block, jnp.arange(Tq // blk))
    return jnp.moveaxis(o, 0, 1).reshape(B, Tq, H, -1).astype(q.dtype)


def _gla(p, s0, alpha_b, alpha_bias, norm_g):
    B, T, _ = p.shape
    f32 = jnp.float32
    q, k, v, xa, gg = jnp.split(p, GL_SPLITS, axis=-1)
    log_a = jax.nn.log_sigmoid((xa @ alpha_b + alpha_bias).astype(f32)) / GL_TAU
    q = _heads(q.astype(f32), GL_HEADS) * GL_DK ** -0.5
    k = _heads(k.astype(f32), GL_HEADS)
    v = _heads(v.astype(f32), GL_HEADS)
    log_a = _heads(log_a, GL_HEADS)
    C = math.gcd(T, GL_CHUNK)
    n = T // C

    def chunks(t):
        return t.reshape(B, n, C, GL_HEADS, t.shape[-1]).transpose(1, 0, 3, 2, 4)

    causal = jnp.tril(jnp.ones((C, C), dtype=bool))[:, :, None]

    def step(S, xs):
        qc, kc, vc, lc = xs
        b = jnp.cumsum(lc, axis=2)
        diff = b[:, :, :, None, :] - b[:, :, None, :, :]
        decay = jnp.exp(jnp.where(causal, diff, -jnp.inf))
        att = jnp.einsum('bhtd,bhsd,bhtsd->bhts', qc, kc, decay)
        o = jnp.einsum('bhts,bhsv->bhtv', att, vc) + jnp.einsum('bhtd,bhdv->bhtv', qc * jnp.exp(b), S)
        b_end = b[:, :, -1:, :]
        S = S * jnp.exp(b_end)[:, :, 0, :, None] + jnp.einsum('bhsd,bhsv->bhdv', kc * jnp.exp(b_end - b), vc)
        return S, o

    S, o = lax.scan(step, s0.astype(f32), (chunks(q), chunks(k), chunks(v), chunks(log_a)))
    o = o.transpose(1, 0, 3, 2, 4).reshape(B, T, GL_HEADS, GL_DV)
    o = _rms(o, norm_g).reshape(B, T, GL_VW)
    return (o * jax.nn.silu(gg.astype(f32))).astype(p.dtype), S.astype(s0.dtype)


def _trunk(x, shift0, rw_s0, gl_s0, paged, params):
    (attn_norm_g, w_in, rw_mu, rw_w_b, rw_w0, rw_a_b, rw_a0, rw_g_b, rw_k_k, rw_k_a, rw_r_k,
     rw_lnx_g, rw_lnx_b, fx_qn_g, fx_kn_g, fx_b_f, gl_alpha_b, gl_alpha_bias, gl_norm_g,
     w_out, ffn_norm_g, w_gu, w_down) = params
    B, T, _ = x.shape
    k_rows, v_rows, lf_rows, rw_states, shifts, gl_states = [], [], [], [], [], []
    for l in range(DEPTH):
        h = _rms(x, attn_norm_g[l])
        proj = h @ w_in[l]
        p_rw, p_fx, p_gl = jnp.split(proj, (RW_COLS, RW_COLS + FX_COLS), axis=-1)
        o_rw, shift_l, rw_l = _rwkv7(p_rw, shift0[:, l], rw_s0[:, l], rw_mu[l], rw_w_b[l], rw_w0[l],
                                     rw_a_b[l], rw_a0[l], rw_g_b[l], rw_k_k[l], rw_k_a[l], rw_r_k[l],
                                     rw_lnx_g[l], rw_lnx_b[l])
        q, k, v, logf = _fox_project(p_fx, fx_qn_g[l], fx_kn_g[l], fx_b_f[l])
        if paged is None:
            k_all, v_all, lf_all = k, v, logf
        else:
            cache_k, cache_v, cache_logf, page_table = paged
            k_all = jnp.concatenate([_gather_pages(cache_k, page_table, l).astype(k.dtype), k], axis=1)
            v_all = jnp.concatenate([_gather_pages(cache_v, page_table, l).astype(v.dtype), v], axis=1)
            lf_all = jnp.concatenate([_gather_pages(cache_logf, page_table, l).astype(logf.dtype), logf], axis=1)
        o_fx = _fox_attend(q, k_all, v_all, lf_all, k_all.shape[1] - T).reshape(B, T, FX_W)
        o_gl, gl_l = _gla(p_gl, gl_s0[:, l], gl_alpha_b[l], gl_alpha_bias[l], gl_norm_g[l])
        x = x + jnp.concatenate([o_rw, o_fx, o_gl], axis=-1) @ w_out[l]
        gate, up = jnp.split(_rms(x, ffn_norm_g[l]) @ w_gu[l], 2, axis=-1)
        x = x + (jax.nn.silu(gate) * up) @ w_down[l]
        k_rows.append(k)
        v_rows.append(v)
        lf_rows.append(logf)
        rw_states.append(rw_l)
        shifts.append(shift_l)
        gl_states.append(gl_l)
    return (x, jnp.stack(k_rows, 1), jnp.stack(v_rows, 1), jnp.stack(lf_rows, 1),
            jnp.stack(rw_states, 1), jnp.stack(shifts, 1), jnp.stack(gl_states, 1))


def setup_inputs(seed: int = 0) -> dict:
    key = jax.random.key(seed)
    ks = jax.random.split(key, 40)
    f32 = jnp.float32
    n_pages = PAST_LEN // PAGE_SIZE
    n_used = DEC_BATCH * n_pages
    n_pool = n_used + (n_used + 3) // 4

    def nrm(k, shape, scale):
        return jax.random.normal(k, shape, f32) * scale

    page_table = jax.random.permutation(ks[0], n_pool)[:n_used].reshape(DEC_BATCH, n_pages).astype(jnp.int32)
    return {
        "x_prompt": nrm(ks[1], (BATCH, SEQ, D_MODEL), 1.0),
        "x_sample": nrm(ks[2], (DEC_BATCH, DEC_SEQ, D_MODEL), 1.0),
        "cache_k": nrm(ks[3], (n_pool, DEPTH, PAGE_SIZE, FX_HEADS, FX_HD), 1.0),
        "cache_v": nrm(ks[4], (n_pool, DEPTH, PAGE_SIZE, FX_HEADS, FX_HD), 1.0),
        "cache_logf": jax.nn.log_sigmoid(nrm(ks[5], (n_pool, DEPTH, PAGE_SIZE, FX_HEADS), 1.0) + 2.0),
        "state_rwkv": nrm(ks[6], (DEC_BATCH, DEPTH, RW_HEADS, RW_HD, RW_HD), 0.1),
        "state_rwkv_shift": nrm(ks[7], (DEC_BATCH, DEPTH, RW_COLS), 1.0),
        "state_gla": nrm(ks[8], (DEC_BATCH, DEPTH, GL_HEADS, GL_DK, GL_DV), 0.1),
        "page_table": page_table,
        "attn_norm_g": 1.0 + nrm(ks[9], (DEPTH, D_MODEL), 0.02),
        "w_in": nrm(ks[10], (DEPTH, D_MODEL, IN_COLS), D_MODEL ** -0.5),
        "rw_mu": jax.random.uniform(ks[11], (DEPTH, RW_COLS), f32, 0.1, 0.9),
        "rw_w_b": nrm(ks[12], (DEPTH, RW_DECAY_LORA, RW_W), RW_DECAY_LORA ** -0.5),
        "rw_w0": -1.0 + nrm(ks[13], (DEPTH, RW_W), 0.5),
        "rw_a_b": nrm(ks[14], (DEPTH, RW_A_LORA, RW_W), RW_A_LORA ** -0.5),
        "rw_a0": nrm(ks[15], (DEPTH, RW_W), 0.1),
        "rw_g_b": nrm(ks[16], (DEPTH, RW_G_LORA, RW_W), RW_G_LORA ** -0.5),
        "rw_k_k": 0.85 + nrm(ks[17], (DEPTH, RW_W), 0.02),
        "rw_k_a": 1.0 + nrm(ks[18], (DEPTH, RW_W), 0.02),
        "rw_r_k": nrm(ks[19], (DEPTH, RW_HEADS, RW_HD), 0.3),
        "rw_lnx_g": 1.0 + nrm(ks[20], (DEPTH, RW_W), 0.02),
        "rw_lnx_b": nrm(ks[21], (DEPTH, RW_W), 0.02),
        "fx_qn_g": 1.0 + nrm(ks[22], (DEPTH, FX_HD), 0.02),
        "fx_kn_g": 1.0 + nrm(ks[23], (DEPTH, FX_HD), 0.02),
        "fx_b_f": 2.0 + nrm(ks[24], (DEPTH, FX_HEADS), 0.5),
        "gl_alpha_b": nrm(ks[25], (DEPTH, GL_GATE_LORA, GL_KW), GL_GATE_LORA ** -0.5),
        "gl_alpha_bias": nrm(ks[26], (DEPTH, GL_KW), 0.1),
        "gl_norm_g": 1.0 + nrm(ks[27], (DEPTH, GL_DV), 0.02),
        "w_out": nrm(ks[28], (DEPTH, MIX_W, D_MODEL), MIX_W ** -0.5),
        "ffn_norm_g": 1.0 + nrm(ks[29], (DEPTH, D_MODEL), 0.02),
        "w_gu": nrm(ks[30], (DEPTH, D_MODEL, 2 * D_FF), D_MODEL ** -0.5),
        "w_down": nrm(ks[31], (DEPTH, D_FF, D_MODEL), D_FF ** -0.5),
    }


def reference(x_prompt, x_sample, cache_k, cache_v, cache_logf, state_rwkv, state_rwkv_shift, state_gla,
              page_table, attn_norm_g, w_in, rw_mu, rw_w_b, rw_w0, rw_a_b, rw_a0, rw_g_b, rw_k_k, rw_k_a,
              rw_r_k, rw_lnx_g, rw_lnx_b, fx_qn_g, fx_kn_g, fx_b_f, gl_alpha_b, gl_alpha_bias, gl_norm_g,
              w_out, ffn_norm_g, w_gu, w_down):
    params = (attn_norm_g, w_in, rw_mu, rw_w_b, rw_w0, rw_a_b, rw_a0, rw_g_b, rw_k_k, rw_k_a, rw_r_k,
              rw_lnx_g, rw_lnx_b, fx_qn_g, fx_kn_g, fx_b_f, gl_alpha_b, gl_alpha_bias, gl_norm_g,
              w_out, ffn_norm_g, w_gu, w_down)
    B = x_prompt.shape[0]
    zero_shift = jnp.zeros((B, DEPTH, RW_COLS), x_prompt.dtype)
    zero_rw = jnp.zeros((B, DEPTH, RW_HEADS, RW_HD, RW_HD), state_rwkv.dtype)
    zero_gl = jnp.zeros((B, DEPTH, GL_HEADS, GL_DK, GL_DV), state_gla.dtype)
    y_prompt, p_k, p_v, p_logf, p_rwkv, p_shift, p_gla = _trunk(
        x_prompt, zero_shift, zero_rw, zero_gl, None, params)
    y_sample, s_k, s_v, s_logf, s_rwkv, s_shift, s_gla = _trunk(
        x_sample, state_rwkv_shift, state_rwkv, state_gla, (cache_k, cache_v, cache_logf, page_table), params)
    return (y_prompt, y_sample, p_k, p_v, p_logf, p_rwkv, p_shift, p_gla,
            s_k, s_v, s_logf, s_rwkv, s_shift, s_gla)
```

```python
import functools

import jax
import jax.numpy as jnp
from jax import lax
from jax.experimental import pallas as pl
from jax.experimental.pallas import tpu as pltpu

F32 = jnp.float32
BF16 = jnp.bfloat16

RW_HD = 64
RW_DECAY_LORA = 32
RW_A_LORA = 32
RW_G_LORA = 96
RW_GN_EPS = 64e-5
FX_HD = 128
GL_HEADS = 4
GL_GATE_LORA = 16
GL_TAU = 16.0
NORM_EPS = 1e-6
PAGE = 128

LANES = 128
VMEM_LIMIT = 56 * 1024 * 1024

NN = (((1,), (0,)), ((), ()))
NT = (((1,), (1,)), ((), ()))
TN = (((0,), (0,)), ((), ()))


def _split2(x):
    hi = x.astype(BF16)
    lo = (x - hi.astype(F32)).astype(BF16)
    return hi, lo


def _split3(x):
    hi = x.astype(BF16)
    r = x - hi.astype(F32)
    mid = r.astype(BF16)
    lo = (r - mid.astype(F32)).astype(BF16)
    return hi, mid, lo


def _mm(a, b, dims=NN, prec=1):
    dot = functools.partial(lax.dot_general, dimension_numbers=dims, preferred_element_type=F32)
    if prec == 1:
        return dot(a.astype(BF16), b.astype(BF16))
    if prec == 3:
        ah, al = _split2(a)
        bh, bl = _split2(b)
        return dot(ah, bh) + (dot(ah, bl) + dot(al, bh))
    if prec == "a":
        bb = b.astype(BF16)
        h, m, l = _split3(a)
        return dot(h, bb) + (dot(m, bb) + dot(l, bb))
    if prec == "b":
        ab = a.astype(BF16)
        h, m, l = _split3(b)
        return dot(ab, h) + (dot(ab, m) + dot(ab, l))
    raise ValueError(prec)


def _sigmoid(x):
    return 1.0 / (1.0 + jnp.exp(-x))


def _softplus(x):
    return jnp.maximum(x, 0.0) + jnp.log1p(jnp.exp(-jnp.abs(x)))


def _iota2(shape, axis):
    return lax.broadcasted_iota(jnp.int32, shape, axis)


def _log2(n):
    assert n & (n - 1) == 0
    return n.bit_length() - 1


def _onehot(mask, dtype=BF16):
    return jnp.where(mask, 1.0, 0.0).astype(dtype)


def _cparams(*sem):
    return pltpu.CompilerParams(dimension_semantics=sem, vmem_limit_bytes=VMEM_LIMIT)


def _dense_kernel(*refs, rms, swiglu, residual, stage):
    it = iter(refs)
    a_ref = next(it)
    g_ref = next(it) if rms else None
    w_ref = next(it)
    w2_ref = next(it) if swiglu else None
    r_ref = next(it) if residual else None
    o_ref = next(it)
    a_sc = next(it) if stage else None

    if stage:
        @pl.when(pl.program_id(1) == 0)
        def _():
            a = a_ref[...].astype(F32)
            if rms:
                a = a * lax.rsqrt(jnp.mean(a * a, axis=-1, keepdims=True) + NORM_EPS) * g_ref[...]
            a_sc[...] = a.astype(BF16)

        a = a_sc[...]
    else:
        a = a_ref[...]
    y = jnp.dot(a, w_ref[...], preferred_element_type=F32)
    if swiglu:
        u = jnp.dot(a, w2_ref[...], preferred_element_type=F32)
        y = y * _sigmoid(y) * u
    if residual:
        y = y + r_ref[...]
    o_ref[...] = y.astype(o_ref.dtype)


def _dense(a, w, *, tm, tn, g=None, residual=None, swiglu=False, out_dtype=F32, name):
    m, k = a.shape
    n = w.shape[1] // 2 if swiglu else w.shape[1]
    assert m % tm == 0 and n % tn == 0
    rms = g is not None
    stage = rms or a.dtype != BF16
    nj = n // tn
    in_specs = [pl.BlockSpec((tm, k), lambda i, j: (i, 0))]
    args = [a]
    if rms:
        in_specs.append(pl.BlockSpec((1, k), lambda i, j: (0, 0)))
        args.append(g.reshape(1, k))
    in_specs.append(pl.BlockSpec((k, tn), lambda i, j: (0, j)))
    args.append(w)
    if swiglu:
        in_specs.append(pl.BlockSpec((k, tn), lambda i, j: (0, j + nj)))
        args.append(w)
    if residual is not None:
        in_specs.append(pl.BlockSpec((tm, tn), lambda i, j: (i, j)))
        args.append(residual)
    return pl.pallas_call(
        functools.partial(_dense_kernel, rms=rms, swiglu=swiglu, residual=residual is not None, stage=stage),
        grid=(m // tm, nj),
        in_specs=in_specs,
        out_specs=pl.BlockSpec((tm, tn), lambda i, j: (i, j)),
        out_shape=jax.ShapeDtypeStruct((m, n), out_dtype),
        scratch_shapes=[pltpu.VMEM((tm, k), BF16)] if stage else [],
        compiler_params=_cparams("arbitrary", "arbitrary"),
        name=name,
    )(*args)


def _segsum64(x, ones_blk):
    parts = [
        _mm(x[:, LANES * i : LANES * (i + 1)], ones_blk, prec="a") for i in range(x.shape[1] // LANES)
    ]
    return jnp.concatenate(parts, axis=1)


def _head_ones():
    r = _iota2((LANES, LANES), 0) >> _log2(RW_HD)
    c = _iota2((LANES, LANES), 1) >> _log2(RW_HD)
    return _onehot(r == c)


def _rw_prep_kernel(
    pr_ref, pk_ref, pv_ref, pz_ref, sr_ref, sk_ref, sv_ref, sz_ref, vec_ref, muz_ref, lora_ref,
    r_o, k_o, v_o, kk_o, bb_o, lw_o, bonus_o, g_o,
    prev_r, prev_k, prev_v, prev_z,
):
    tb = pr_ref.shape[0]

    @pl.when(pl.program_id(1) == 0)
    def _():
        prev_r[...] = sr_ref[0]
        prev_k[...] = sk_ref[0]
        prev_v[...] = sv_ref[0]
        prev_z[...] = sz_ref[0]

    def shift(p_ref, prev, mu):
        p = p_ref[...]
        row = _iota2(p.shape, 0)
        p_prev = jnp.where(row == 0, prev[...], pltpu.roll(p, 1, 0))
        prev[...] = p[tb - 1 : tb, :]
        return p + mu * (p_prev - p)

    vec = vec_ref[...]
    w0, a0, k_k, k_a, r_k = vec[0:1], vec[1:2], vec[2:3], vec[3:4], vec[4:5]
    r = shift(pr_ref, prev_r, vec[5:6])
    k = shift(pk_ref, prev_k, vec[6:7])
    v = shift(pv_ref, prev_v, vec[7:8])
    z = shift(pz_ref, prev_z, muz_ref[...])

    lane = _iota2(z.shape, 1)
    in_w = lane < RW_DECAY_LORA
    in_a = (lane >= RW_DECAY_LORA) & (lane < RW_DECAY_LORA + RW_A_LORA)
    lora = lora_ref[...]
    lw_lin = _mm(jnp.where(in_w, jnp.tanh(z), 0.0), lora, prec=3)
    a_lin = _mm(jnp.where(in_a, z, 0.0), lora, prec=3)
    g = _mm(jnp.where(in_w | in_a, 0.0, _sigmoid(z)), lora, prec=3)

    lw = -jnp.exp(-_softplus(-(w0 + lw_lin)) - 0.5)
    a = _sigmoid(a0 + a_lin)
    ones_blk = _head_ones()
    kk = k * k_k
    kk = kk / jnp.maximum(jnp.sqrt(_segsum64(kk * kk, ones_blk)), 1e-12)
    k = k * (1.0 + (a - 1.0) * k_a)
    r_o[...] = r
    k_o[...] = k
    v_o[...] = v
    kk_o[...] = kk
    bb_o[...] = a * kk
    lw_o[...] = lw
    bonus_o[...] = _segsum64(r * k * r_k, ones_blk) * v
    g_o[...] = g


def _rw_prep(proj, col0, col_z, shift0, vec, muz, lora, *, nseq, t, tb, name):
    w = 512
    nt = t // tb
    row = lambda b, i: b * nt + i
    sr, sk, sv, sz = shift0
    in_specs = [
        pl.BlockSpec((tb, w), lambda b, i: (row(b, i), col0)),
        pl.BlockSpec((tb, w), lambda b, i: (row(b, i), col0 + 1)),
        pl.BlockSpec((tb, w), lambda b, i: (row(b, i), col0 + 2)),
        pl.BlockSpec((tb, 256), lambda b, i: (row(b, i), col_z)),
        pl.BlockSpec((1, 1, w), lambda b, i: (b, 0, 0)),
        pl.BlockSpec((1, 1, w), lambda b, i: (b, 0, 0)),
        pl.BlockSpec((1, 1, w), lambda b, i: (b, 0, 0)),
        pl.BlockSpec((1, 1, 256), lambda b, i: (b, 0, 0)),
        pl.BlockSpec((8, w), lambda b, i: (0, 0)),
        pl.BlockSpec((1, 256), lambda b, i: (0, 0)),
        pl.BlockSpec((256, w), lambda b, i: (0, 0)),
    ]
    out = jax.ShapeDtypeStruct((nseq * t, w), F32)
    return pl.pallas_call(
        _rw_prep_kernel,
        grid=(nseq, nt),
        in_specs=in_specs,
        out_specs=[pl.BlockSpec((tb, w), lambda b, i: (row(b, i), 0))] * 8,
        out_shape=[out] * 8,
        scratch_shapes=[pltpu.VMEM((1, w), F32)] * 3 + [pltpu.VMEM((1, 256), F32)],
        compiler_params=_cparams("arbitrary", "arbitrary"),
        name=name,
    )(proj, proj, proj, proj, sr, sk, sv, sz, vec, muz, lora)


RW_PREC = 3


def _inv_unit_lower(l_mat, c):
    ti = _iota2((c, c), 0)
    si = _iota2((c, c), 1)
    t_mat = _onehot(ti == si, F32)
    lev = 0
    while (1 << lev) < c:
        off = ((ti >> (lev + 1)) == (si >> (lev + 1))) & ((ti >> lev) != (si >> lev))
        l_off = jnp.where(off, l_mat, 0.0)
        t_mat = t_mat - _mm(_mm(t_mat, l_off, prec=RW_PREC), t_mat, prec=RW_PREC)
        lev += 1
    return t_mat


def _rw_scan_kernel(
    r_ref, k_ref, v_ref, kk_ref, bb_ref, lw_ref, bonus_ref, g_ref, ln_ref, s0_ref,
    o_ref, st_ref, a_sc, *, c, valid, nchunks,
):
    ci = pl.program_id(1)

    @pl.when(ci == 0)
    def _():
        a_sc[...] = s0_ref[0]

    ti = _iota2((c, c), 0)
    si = _iota2((c, c), 1)
    incl = si <= ti
    strict = si < ti
    tri = _onehot(incl)
    ri = _iota2((LANES, LANES), 0)
    cj = _iota2((LANES, LANES), 1)
    eye = ri == cj
    blockdiag = (ri >> _log2(RW_HD)) == (cj >> _log2(RW_HD))
    ones_blk = _onehot(blockdiag)
    lane_head = _iota2((1, LANES), 1) >> _log2(RW_HD)
    live = None
    if valid < c:
        live = (_iota2((c, 1), 0) + ci * c) < valid
    mm = functools.partial(_mm, prec=RW_PREC)

    for p in range(r_ref.shape[1] // LANES):
        sl = slice(LANES * p, LANES * (p + 1))
        r, k, v = r_ref[:, sl], k_ref[:, sl], v_ref[:, sl]
        kk, bb, lw = kk_ref[:, sl], bb_ref[:, sl], lw_ref[:, sl]
        if live is not None:
            k = jnp.where(live, k, 0.0)
            kk = jnp.where(live, kk, 0.0)
            bb = jnp.where(live, bb, 0.0)
            lw = jnp.where(live, lw, 0.0)
        lwc = _mm(tri, lw, prec="b")
        lwe = lwc[c - 1 : c, :]
        e_neg = jnp.exp(-lwc)
        e_end = jnp.exp(lwe - lwc)
        a_t = kk * jnp.exp(lwc - lw)
        k_t = k * e_neg
        b_t = bb * e_neg
        r_t = r * jnp.exp(lwc)
        k_h = k * e_end
        b_h = bb * e_end
        ta = tmv = qta = yc = None
        for h in range(LANES // RW_HD):
            mh = lane_head == h
            a_m = jnp.where(mh, a_t, 0.0)
            r_m = jnp.where(mh, r_t, 0.0)
            v_m = jnp.where(mh, v, 0.0)
            l_mat = jnp.where(strict, mm(a_m, b_t, NT), 0.0)
            m_mat = jnp.where(strict, mm(a_m, k_t, NT), 0.0)
            p_mat = jnp.where(incl, mm(r_m, k_t, NT), 0.0)
            q_mat = jnp.where(incl, mm(r_m, b_t, NT), 0.0)
            t_mat = _inv_unit_lower(l_mat, c)
            ta_h = mm(t_mat, a_m)
            tmv_h = mm(t_mat, mm(m_mat, v_m))
            qta_h = mm(q_mat, ta_h)
            yc_h = mm(p_mat, v_m) - mm(q_mat, tmv_h)
            ta = ta_h if ta is None else ta + ta_h
            tmv = tmv_h if tmv is None else tmv + tmv_h
            qta = qta_h if qta is None else qta + qta_h
            yc = yc_h if yc is None else yc + yc_h
        ry = r_t - qta
        g_mat = jnp.where(eye, jnp.exp(lwe), 0.0) - jnp.where(blockdiag, mm(b_h, ta, TN), 0.0)
        h_mat = jnp.where(blockdiag, mm(k_h, v, TN) - mm(b_h, tmv, TN), 0.0)
        a0 = a_sc[p]
        y = mm(ry, a0) + yc
        a_sc[p] = mm(g_mat, a0) + h_mat
        mean = _mm(y, ones_blk, prec="a") * (1.0 / RW_HD)
        d = y - mean
        var = _mm(d * d, ones_blk, prec="a") * (1.0 / RW_HD)
        yn = d * lax.rsqrt(var + RW_GN_EPS)
        o_ref[:, sl] = (yn * ln_ref[0:1, sl] + ln_ref[1:2, sl] + bonus_ref[:, sl]) * g_ref[:, sl]

    @pl.when(ci == nchunks - 1)
    def _():
        st_ref[0] = a_sc[...]


def _rw_scan(prep, ln, s0, *, nseq, t, c, valid, name):
    w = 512
    nchunks = t // c
    npair = w // LANES
    blk = pl.BlockSpec((c, w), lambda b, i: (b * nchunks + i, 0))
    st_spec = pl.BlockSpec((1, npair, LANES, LANES), lambda b, i: (b, 0, 0, 0))
    return pl.pallas_call(
        functools.partial(_rw_scan_kernel, c=c, valid=valid, nchunks=nchunks),
        grid=(nseq, nchunks),
        in_specs=[blk] * 8 + [pl.BlockSpec((2, w), lambda b, i: (0, 0)), st_spec],
        out_specs=[blk, st_spec],
        out_shape=[
            jax.ShapeDtypeStruct((nseq * t, w), F32),
            jax.ShapeDtypeStruct((nseq, npair, LANES, LANES), F32),
        ],
        scratch_shapes=[pltpu.VMEM((npair, LANES, LANES), F32)],
        compiler_params=_cparams("arbitrary", "arbitrary"),
        name=name,
    )(*prep, ln, s0)


GL_PREC = 3


def _gla_kernel(
    q_ref, k_ref, v_ref, xa_ref, gg_ref, ab_ref, abias_ref, ng_ref, s0_ref,
    o_ref, st_ref, s_sc, *, tb, valid, nblocks, dk,
):
    bi = pl.program_id(1)

    @pl.when(bi == 0)
    def _():
        s_sc[...] = s0_ref[0]

    mm = functools.partial(_mm, prec=GL_PREC)
    ti = _iota2((tb, tb), 0)
    si = _iota2((tb, tb), 1)
    tri = _onehot(si <= ti)
    trow = _iota2((tb, 1), 0)
    lane_head = _iota2((1, LANES), 1) >> _log2(dk)
    heads_per_tile = LANES // dk

    la_all = -_softplus(-(mm(xa_ref[...], ab_ref[...]) + abias_ref[...])) * (1.0 / GL_TAU)
    k_all = k_ref[...]
    if valid < tb:
        live = (trow + bi * tb) < valid
        la_all = jnp.where(live, la_all, 0.0)
        k_all = jnp.where(live, k_all, 0.0)

    for p in range(q_ref.shape[1] // LANES):
        sl = slice(LANES * p, LANES * (p + 1))
        q = q_ref[:, sl] * (dk**-0.5)
        k = k_all[:, sl]
        b = _mm(tri, la_all[:, sl], prec="b")
        b_end = b[tb - 1 : tb, :]
        att = [None] * heads_per_tile

        def add_level(qs, ks, keep):
            for h in range(heads_per_tile):
                s = jnp.where(keep, mm(jnp.where(lane_head == h, qs, 0.0), ks, NT), 0.0)
                att[h] = s if att[h] is None else att[h] + s

        add_level(q, k, ti == si)
        lev = 1
        while (1 << lev) <= tb:
            half = 1 << (lev - 1)
            sel = _onehot(si == ((ti >> lev) << lev) + (half - 1))
            b_ref_rows = _mm(sel, b, prec="b")
            upper = ((trow >> (lev - 1)) & 1) == 1
            qs = jnp.where(upper, q * jnp.exp(jnp.minimum(b - b_ref_rows, 0.0)), 0.0)
            ks = jnp.where(upper, 0.0, k * jnp.exp(jnp.minimum(b_ref_rows - b, 0.0)))
            add_level(qs, ks, (ti >> lev) == (si >> lev))
            lev += 1

        qe = q * jnp.exp(b)
        ke = k * jnp.exp(b_end - b)
        s_old = s_sc[p]
        s_new = s_old * jnp.exp(b_end)
        for h in range(heads_per_tile):
            hv = heads_per_tile * p + h
            vsl = slice(LANES * hv, LANES * (hv + 1))
            v = v_ref[:, vsl]
            mh = lane_head == h
            o = mm(att[h], v) + mm(jnp.where(mh, qe, 0.0), s_old, NT)
            s_new = s_new + mm(v, jnp.where(mh, ke, 0.0), TN)
            o = o * lax.rsqrt(jnp.mean(o * o, axis=-1, keepdims=True) + NORM_EPS) * ng_ref[...]
            gg = gg_ref[:, vsl]
            o_ref[:, vsl] = o * (gg * _sigmoid(gg))
        s_sc[p] = s_new

    @pl.when(bi == nblocks - 1)
    def _():
        st_ref[0] = s_sc[...]


def _gla(proj, cols, ab, abias, ng, s0, *, nseq, t, tb, valid, name):
    cq, ck, cv, cxa, cgg = cols
    nb = t // tb
    row = lambda b, i: b * nb + i
    npair = 256 // LANES
    st_spec = pl.BlockSpec((1, npair, LANES, LANES), lambda b, i: (b, 0, 0, 0))
    return pl.pallas_call(
        functools.partial(_gla_kernel, tb=tb, valid=valid, nblocks=nb, dk=256 // GL_HEADS),
        grid=(nseq, nb),
        in_specs=[
            pl.BlockSpec((tb, 256), lambda b, i: (row(b, i), cq)),
            pl.BlockSpec((tb, 256), lambda b, i: (row(b, i), ck)),
            pl.BlockSpec((tb, 512), lambda b, i: (row(b, i), cv)),
            pl.BlockSpec((tb, LANES), lambda b, i: (row(b, i), cxa)),
            pl.BlockSpec((tb, 512), lambda b, i: (row(b, i), cgg)),
            pl.BlockSpec((LANES, 256), lambda b, i: (0, 0)),
            pl.BlockSpec((1, 256), lambda b, i: (0, 0)),
            pl.BlockSpec((1, LANES), lambda b, i: (0, 0)),
            st_spec,
        ],
        out_specs=[pl.BlockSpec((tb, 512), lambda b, i: (row(b, i), 0)), st_spec],
        out_shape=[
            jax.ShapeDtypeStruct((nseq * t, 512), F32),
            jax.ShapeDtypeStruct((nseq, npair, LANES, LANES), F32),
        ],
        scratch_shapes=[pltpu.VMEM((npair, LANES, LANES), F32)],
        compiler_params=_cparams("arbitrary", "arbitrary"),
        name=name,
    )(proj, proj, proj, proj, proj, ab, abias, ng, s0)


def _fx_prep_kernel(q_ref, k_ref, v_ref, fl_ref, qg_ref, kg_ref, bf_ref,
                    qb_o, kf_o, kb_o, vb_o, lf_o, c_o, carry):
    tb = q_ref.shape[0]

    @pl.when(pl.program_id(1) == 0)
    def _():
        carry[...] = jnp.zeros_like(carry)

    def headnorm(x, gain):
        return x * lax.rsqrt(jnp.mean(x * x, axis=-1, keepdims=True) + NORM_EPS) * gain

    for h in range(q_ref.shape[1] // FX_HD):
        sl = slice(FX_HD * h, FX_HD * (h + 1))
        qb_o[:, sl] = (headnorm(q_ref[:, sl], qg_ref[...]) * (FX_HD**-0.5)).astype(BF16)
        kn = headnorm(k_ref[:, sl], kg_ref[...])
        kf_o[:, sl] = kn
        kb_o[:, sl] = kn.astype(BF16)
    vb_o[...] = v_ref[...].astype(BF16)
    logf = -_softplus(-(fl_ref[...] + bf_ref[...]))
    lf_o[...] = logf
    tri = _onehot(_iota2((tb, tb), 1) <= _iota2((tb, tb), 0))
    c = _mm(tri, logf, prec="b") + carry[...]
    c_o[...] = c
    carry[...] = c[tb - 1 : tb, :]


def _fx_prep(proj, cols, qg, kg, bf, *, nseq, t, tb, name):
    cq, ck, cv, cfl = cols
    nt = t // tb
    row = lambda b, i: b * nt + i
    w = 1024
    wide = pl.BlockSpec((tb, w), lambda b, i: (row(b, i), 0))
    thin = pl.BlockSpec((tb, LANES), lambda b, i: (row(b, i), 0))
    vecspec = pl.BlockSpec((1, LANES), lambda b, i: (0, 0))
    rows = nseq * t
    return pl.pallas_call(
        _fx_prep_kernel,
        grid=(nseq, nt),
        in_specs=[
            pl.BlockSpec((tb, w), lambda b, i: (row(b, i), cq)),
            pl.BlockSpec((tb, w), lambda b, i: (row(b, i), ck)),
            pl.BlockSpec((tb, w), lambda b, i: (row(b, i), cv)),
            pl.BlockSpec((tb, LANES), lambda b, i: (row(b, i), cfl)),
            vecspec, vecspec, vecspec,
        ],
        out_specs=[wide, wide, wide, wide, thin, thin],
        out_shape=[
            jax.ShapeDtypeStruct((rows, w), BF16),
            jax.ShapeDtypeStruct((rows, w), F32),
            jax.ShapeDtypeStruct((rows, w), BF16),
            jax.ShapeDtypeStruct((rows, w), BF16),
            jax.ShapeDtypeStruct((rows, LANES), F32),
            jax.ShapeDtypeStruct((rows, LANES), F32),
        ],
        scratch_shapes=[pltpu.VMEM((1, LANES), F32)],
        compiler_params=_cparams("arbitrary", "arbitrary"),
        name=name,
    )(proj, proj, proj, proj, qg, kg, bf)


def _flash_kernel(q_ref, k_ref, v_ref, c_ref, cref_ref, o_ref, m_sc, l_sc, acc_sc, *, tq, tk):
    qi = pl.program_id(2)
    kj = pl.program_id(3)

    @pl.when(kj == 0)
    def _():
        m_sc[...] = jnp.full_like(m_sc, -jnp.inf)
        l_sc[...] = jnp.zeros_like(l_sc)
        acc_sc[...] = jnp.zeros_like(acc_sc)

    @pl.when(kj * tk <= qi * tq + (tq - 1))
    def _():
        s = lax.dot_general(q_ref[...], k_ref[...], NT, preferred_element_type=F32)
        s = s + (cref_ref[0, 0, :, 0:1] - c_ref[0, 0])
        qpos = qi * tq + _iota2((tq, tk), 0)
        kpos = kj * tk + _iota2((tq, tk), 1)
        s = jnp.where(kpos <= qpos, s, -jnp.inf)
        m_new = jnp.maximum(m_sc[...], jnp.max(s, axis=-1, keepdims=True))
        alpha = jnp.exp(m_sc[...] - m_new)
        p = jnp.exp(s - m_new)
        l_sc[...] = alpha * l_sc[...] + jnp.sum(p, axis=-1, keepdims=True)
        acc_sc[...] = alpha * acc_sc[...] + jnp.dot(p.astype(BF16), v_ref[...], preferred_element_type=F32)
        m_sc[...] = m_new

    @pl.when(kj == pl.num_programs(3) - 1)
    def _():
        o_ref[...] = acc_sc[...] / l_sc[...]


def _flash(qb, kb, vb, c_row, *, nseq, t, tq, tk, name):
    nh = qb.shape[1] // FX_HD
    nq, nk = t // tq, t // tk
    last = lambda i: (i * tq + tq - 1) // tk
    return pl.pallas_call(
        functools.partial(_flash_kernel, tq=tq, tk=tk),
        grid=(nseq, nh, nq, nk),
        in_specs=[
            pl.BlockSpec((tq, FX_HD), lambda b, h, i, j: (b * nq + i, h)),
            pl.BlockSpec((tk, FX_HD), lambda b, h, i, j: (b * nk + jnp.minimum(j, last(i)), h)),
            pl.BlockSpec((tk, FX_HD), lambda b, h, i, j: (b * nk + jnp.minimum(j, last(i)), h)),
            pl.BlockSpec((1, 1, 1, tk), lambda b, h, i, j: (b, h, 0, jnp.minimum(j, last(i)))),
            pl.BlockSpec((1, 1, 1, LANES), lambda b, h, i, j: (b, h, 0, i * (tq // LANES))),
        ],
        out_specs=pl.BlockSpec((tq, FX_HD), lambda b, h, i, j: (b * nq + i, h)),
        out_shape=jax.ShapeDtypeStruct((nseq * t, nh * FX_HD), F32),
        scratch_shapes=[pltpu.VMEM((tq, 1), F32), pltpu.VMEM((tq, 1), F32), pltpu.VMEM((tq, FX_HD), F32)],
        compiler_params=_cparams("arbitrary", "arbitrary", "arbitrary", "arbitrary"),
        name=name,
    )(qb, kb, vb, c_row, c_row)


def _decode_kernel(pt_ref, q_ref, kn_ref, vn_ref, cn_ref, kc_ref, vc_ref, lf_ref, o_ref,
                   m_sc, l_sc, acc_sc, carry, *, nq, nh):
    j = pl.program_id(1)

    @pl.when(j == 0)
    def _():
        m_sc[...] = jnp.full_like(m_sc, -jnp.inf)
        l_sc[...] = jnp.zeros_like(l_sc)
        acc_sc[...] = jnp.zeros_like(acc_sc)
        carry[...] = jnp.zeros_like(carry)

    def attend(k_page, v_page, bias, causal):
        for h in range(nh):
            sl = slice(FX_HD * h, FX_HD * (h + 1))
            s = lax.dot_general(q_ref[:, sl], k_page[:, sl], NT, preferred_element_type=F32)
            s = s + bias[h : h + 1, :]
            if causal:
                s = jnp.where(_iota2(s.shape, 1) <= _iota2(s.shape, 0), s, -jnp.inf)
            m_old = m_sc[h]
            m_new = jnp.maximum(m_old, jnp.max(s, axis=-1, keepdims=True))
            alpha = jnp.exp(m_old - m_new)
            p = jnp.exp(s - m_new)
            l_sc[h] = alpha * l_sc[h] + jnp.sum(p, axis=-1, keepdims=True)
            acc_sc[h] = alpha * acc_sc[h] + jnp.dot(p.astype(BF16), v_page[:, sl], preferred_element_type=F32)
            m_sc[h] = m_new

    @pl.when(j == 0)
    def _():
        attend(kn_ref[...], vn_ref[...], -cn_ref[0], True)

    @pl.when(j > 0)
    def _():
        page = PAGE
        after = _onehot(_iota2((page, page), 0) > _iota2((page, page), 1))
        both = jnp.concatenate([after, jnp.ones((page, page), BF16)], axis=1)
        sums = _mm(lf_ref[...], both, TN, prec="a")
        bias = carry[...] + sums[:, :page]
        carry[...] = carry[...] + sums[:, page:]
        attend(kc_ref[...].astype(BF16), vc_ref[...].astype(BF16), bias, False)

    @pl.when(j == pl.num_programs(1) - 1)
    def _():
        for h in range(nh):
            o_ref[:, FX_HD * h : FX_HD * (h + 1)] = acc_sc[h] / l_sc[h]


def _decode(qb, kn_pad, vn_pad, cn_t, cache_k, cache_v, cache_logf, page_table, layer, *, name):
    nseq, npages = page_table.shape
    nq = DEC_Q_ROWS
    w = qb.shape[1]
    nh = w // FX_HD
    pool, depth = cache_k.shape[0], cache_k.shape[1]
    ck = cache_k.reshape(pool, depth, PAGE, w)
    cv = cache_v.reshape(pool, depth, PAGE, w)

    def page_of(b, j, pt):
        return pt[b * npages + (npages - jnp.maximum(j, 1))]

    grid_spec = pltpu.PrefetchScalarGridSpec(
        num_scalar_prefetch=1,
        grid=(nseq, npages + 1),
        in_specs=[
            pl.BlockSpec((nq, w), lambda b, j, pt: (b * (PAGE // nq), 0)),
            pl.BlockSpec((PAGE, w), lambda b, j, pt: (b, 0)),
            pl.BlockSpec((PAGE, w), lambda b, j, pt: (b, 0)),
            pl.BlockSpec((1, nh, PAGE), lambda b, j, pt: (b, 0, 0)),
            pl.BlockSpec((None, None, PAGE, w), lambda b, j, pt: (page_of(b, j, pt), layer, 0, 0)),
            pl.BlockSpec((None, None, PAGE, w), lambda b, j, pt: (page_of(b, j, pt), layer, 0, 0)),
            pl.BlockSpec((None, None, PAGE, nh), lambda b, j, pt: (page_of(b, j, pt), layer, 0, 0)),
        ],
        out_specs=pl.BlockSpec((nq, w), lambda b, j, pt: (b, 0)),
        scratch_shapes=[
            pltpu.VMEM((nh, nq, 1), F32),
            pltpu.VMEM((nh, nq, 1), F32),
            pltpu.VMEM((nh, nq, FX_HD), F32),
            pltpu.VMEM((nh, PAGE), F32),
        ],
    )
    return pl.pallas_call(
        functools.partial(_decode_kernel, nq=nq, nh=nh),
        grid_spec=grid_spec,
        out_shape=jax.ShapeDtypeStruct((nseq * nq, w), F32),
        compiler_params=_cparams("arbitrary", "arbitrary"),
        name=name,
    )(page_table.reshape(-1), qb, kn_pad, vn_pad, cn_t, ck, cv, cache_logf)


RW_CHUNK = 64
GL_BLOCK = 128
DEC_Q_ROWS = 16
TM = 768
COL_FX_Q, COL_FX_K, COL_FX_V = 0, 1024, 2048
COL_RW_R = 3072
COL_GL_V, COL_GL_GG = 4608, 5120
COL_GL_Q, COL_GL_K = 5632, 5888
COL_RW_Z = 6144
COL_FX_FL = 6400
COL_GL_XA = 6528
COLS_PAD = 6656


def _pad_cols(x, width):
    return jnp.pad(x, [(0, 0)] * (x.ndim - 1) + [(0, width - x.shape[-1])])


def _relayout_in(w, rw_w, fx_w, gl_kw, gl_vw, nh_fx):
    rw_z = RW_DECAY_LORA + RW_A_LORA + RW_G_LORA
    o = 0
    rw_rkv = w[..., o : o + 3 * rw_w]; o += 3 * rw_w
    rw_zz = w[..., o : o + rw_z]; o += rw_z
    fx_qkv = w[..., o : o + 3 * fx_w]; o += 3 * fx_w
    fx_fl = w[..., o : o + nh_fx]; o += nh_fx
    gl_qk = w[..., o : o + 2 * gl_kw]; o += 2 * gl_kw
    gl_v = w[..., o : o + gl_vw]; o += gl_vw
    gl_xa = w[..., o : o + GL_GATE_LORA]; o += GL_GATE_LORA
    gl_gg = w[..., o : o + gl_vw]; o += gl_vw
    assert o == w.shape[-1]
    return jnp.concatenate(
        [fx_qkv, rw_rkv, gl_v, gl_gg, gl_qk, _pad_cols(rw_zz, 256), _pad_cols(fx_fl, LANES), _pad_cols(gl_xa, LANES)],
        axis=-1,
    )


def _mixers(proj, layer, lp, states, paged, *, nseq, t, valid, tag):
    (shift0, rw_s0, gl_s0) = states
    tb = min(t, 256)
    prep = _rw_prep(proj, COL_RW_R // 512, COL_RW_Z // 256, shift0, lp["rw_vec"], lp["rw_muz"], lp["rw_lora"],
                    nseq=nseq, t=t, tb=tb, name=f"rw_prep_{tag}")
    o_rw, rw_state = _rw_scan(prep, lp["rw_ln"], rw_s0, nseq=nseq, t=t, c=min(t, RW_CHUNK), valid=valid,
                              name=f"rw_scan_{tag}")
    o_gl, gl_state = _gla(
        proj, (COL_GL_Q // 256, COL_GL_K // 256, COL_GL_V // 512, COL_GL_XA // LANES, COL_GL_GG // 512),
        lp["gl_ab"], lp["gl_abias"], lp["gl_ng"], gl_s0, nseq=nseq, t=t, tb=min(t, GL_BLOCK), valid=valid,
        name=f"gla_{tag}")
    qb, kf, kb, vb, logf, c = _fx_prep(
        proj, (COL_FX_Q // 1024, COL_FX_K // 1024, COL_FX_V // 1024, COL_FX_FL // LANES),
        lp["fx_qg"], lp["fx_kg"], lp["fx_bf"], nseq=nseq, t=t, tb=tb, name=f"fx_prep_{tag}")
    nh = kf.shape[1] // FX_HD
    c_t = jnp.transpose(c.reshape(nseq, t, LANES)[:, :, :nh], (0, 2, 1))
    if paged is None:
        blk = min(t, 512)
        o_fx = _flash(qb, kb, vb, c_t[:, :, None, :], nseq=nseq, t=t, tq=blk, tk=blk, name=f"flash_{tag}")
    else:
        cache_k, cache_v, cache_logf, page_table = paged
        assert t == PAGE and valid <= DEC_Q_ROWS
        o_fx = _decode(qb, kb, vb, c_t, cache_k, cache_v, cache_logf, page_table, layer, name=f"decode_{tag}")
        o_fx = o_fx.reshape(nseq, DEC_Q_ROWS, -1)[:, :valid].reshape(nseq * valid, -1)
    return o_rw, o_fx, o_gl, kf, logf, rw_state, gl_state


def kernel(x_prompt, x_sample, cache_k, cache_v, cache_logf, state_rwkv, state_rwkv_shift, state_gla,
           page_table, attn_norm_g, w_in, rw_mu, rw_w_b, rw_w0, rw_a_b, rw_a0, rw_g_b, rw_k_k, rw_k_a,
           rw_r_k, rw_lnx_g, rw_lnx_b, fx_qn_g, fx_kn_g, fx_b_f, gl_alpha_b, gl_alpha_bias, gl_norm_g,
           w_out, ffn_norm_g, w_gu, w_down):
    nb, seq, d_model = x_prompt.shape
    ns, dseq, _ = x_sample.shape
    depth = w_in.shape[0]
    rw_w = rw_w0.shape[1]
    rw_heads = rw_w // RW_HD
    nh_fx = fx_b_f.shape[1]
    fx_w = nh_fx * FX_HD
    gl_kw = gl_alpha_b.shape[2]
    gl_dk = gl_kw // GL_HEADS
    gl_dv = gl_norm_g.shape[1]
    gl_vw = GL_HEADS * gl_dv
    rw_cols = 3 * rw_w + RW_DECAY_LORA + RW_A_LORA + RW_G_LORA
    assert (rw_w, fx_w, gl_kw, gl_vw, FX_HD, gl_dv) == (512, 1024, 256, 512, 128, 128)
    n_prompt = nb * seq
    n_sample = ns * dseq
    rows = n_prompt + n_sample
    tm = TM if rows >= TM else 64
    rows_pad = -(-rows // tm) * tm
    relayout = functools.partial(_relayout_in, rw_w=rw_w, fx_w=fx_w, gl_kw=gl_kw, gl_vw=gl_vw, nh_fx=nh_fx)

    w_in_p = relayout(w_in).astype(BF16)
    w_out_b = w_out.astype(BF16)
    w_gu_b = w_gu.astype(BF16)
    w_down_b = w_down.astype(BF16)
    d_ff = w_down.shape[1]
    rw_mu_r, rw_mu_k, rw_mu_v, rw_mu_z = (
        rw_mu[:, :rw_w], rw_mu[:, rw_w : 2 * rw_w], rw_mu[:, 2 * rw_w : 3 * rw_w], rw_mu[:, 3 * rw_w :])
    layers = []
    for l in range(depth):
        layers.append(dict(
            rw_vec=jnp.stack([rw_w0[l], rw_a0[l], rw_k_k[l], rw_k_a[l], rw_r_k[l].reshape(-1),
                              rw_mu_r[l], rw_mu_k[l], rw_mu_v[l]]),
            rw_muz=_pad_cols(rw_mu_z[l][None], 256),
            rw_lora=jnp.pad(jnp.concatenate([rw_w_b[l], rw_a_b[l], rw_g_b[l]], axis=0),
                            ((0, 256 - (rw_cols - 3 * rw_w)), (0, 0))),
            rw_ln=jnp.stack([rw_lnx_g[l], rw_lnx_b[l]]),
            fx_qg=fx_qn_g[l][None], fx_kg=fx_kn_g[l][None], fx_bf=_pad_cols(fx_b_f[l][None], LANES),
            gl_ab=jnp.pad(gl_alpha_b[l], ((0, LANES - GL_GATE_LORA), (0, 0))),
            gl_abias=gl_alpha_bias[l][None], gl_ng=gl_norm_g[l][None],
        ))

    def rw_state_in(s):
        n = s.shape[0]
        a = jnp.swapaxes(s, -1, -2).reshape(n, rw_heads // 2, 2, RW_HD, RW_HD)
        z = jnp.zeros_like(a[:, :, 0])
        top = jnp.concatenate([a[:, :, 0], z], axis=-1)
        bot = jnp.concatenate([z, a[:, :, 1]], axis=-1)
        return jnp.concatenate([top, bot], axis=-2)

    def rw_state_out(a):
        n = a.shape[0]
        h0 = a[:, :, :RW_HD, :RW_HD]
        h1 = a[:, :, RW_HD:, RW_HD:]
        return jnp.swapaxes(jnp.stack([h0, h1], axis=2).reshape(n, rw_heads, RW_HD, RW_HD), -1, -2)

    def gl_state_in(s):
        n = s.shape[0]
        a = jnp.swapaxes(s, -1, -2).reshape(n, GL_HEADS // 2, 2, gl_dv, gl_dk)
        return jnp.transpose(a, (0, 1, 3, 2, 4)).reshape(n, GL_HEADS // 2, gl_dv, 2 * gl_dk)

    def gl_state_out(a):
        n = a.shape[0]
        a = jnp.transpose(a.reshape(n, GL_HEADS // 2, gl_dv, 2, gl_dk), (0, 1, 3, 2, 4))
        return jnp.swapaxes(a.reshape(n, GL_HEADS, gl_dv, gl_dk), -1, -2)

    def shift_in(s):
        return (s[:, None, :rw_w], s[:, None, rw_w : 2 * rw_w], s[:, None, 2 * rw_w : 3 * rw_w],
                _pad_cols(s[:, None, 3 * rw_w :], 256))

    def shift_out(proj_rows):
        return jnp.concatenate([proj_rows[:, COL_RW_R : COL_RW_R + 3 * rw_w],
                                proj_rows[:, COL_RW_Z : COL_RW_Z + rw_cols - 3 * rw_w]], axis=-1)

    x = jnp.concatenate([x_prompt.reshape(n_prompt, d_model), x_sample.reshape(n_sample, d_model),
                         jnp.zeros((rows_pad - rows, d_model), x_prompt.dtype)], axis=0)
    t_s = PAGE
    p_states = (shift_in(jnp.zeros((nb, rw_cols), F32)), jnp.zeros((nb, rw_heads // 2, LANES, LANES), F32),
                jnp.zeros((nb, GL_HEADS // 2, LANES, LANES), F32))
    outs = {k: [] for k in ("pk", "pv", "plf", "prw", "psh", "pgl", "sk", "sv", "slf", "srw", "ssh", "sgl")}
    for l in range(depth):
        lp = layers[l]
        proj = _dense(x, w_in_p[l], tm=tm, tn=512, g=attn_norm_g[l], name=f"w_in_{l}")
        proj_s = jnp.pad(proj[n_prompt:rows].reshape(ns, dseq, COLS_PAD), ((0, 0), (0, t_s - dseq), (0, 0)))
        proj_s = proj_s.reshape(ns * t_s, COLS_PAD)
        s_states = (shift_in(state_rwkv_shift[:, l]), rw_state_in(state_rwkv[:, l]), gl_state_in(state_gla[:, l]))
        po = _mixers(proj, l, lp, p_states, None, nseq=nb, t=seq, valid=seq, tag=f"p{l}")
        so = _mixers(proj_s, l, lp, s_states, (cache_k, cache_v, cache_logf, page_table),
                     nseq=ns, t=t_s, valid=dseq, tag=f"s{l}")

        def unpad(a):
            return a.reshape(ns, t_s, -1)[:, :dseq].reshape(n_sample, -1)

        mix = jnp.concatenate([
            jnp.concatenate([po[0], po[1], po[2]], axis=1),
            jnp.concatenate([unpad(so[0]), so[1], unpad(so[2])], axis=1),
            jnp.zeros((rows_pad - rows, d_model), F32)], axis=0)
        x = _dense(mix, w_out_b[l], tm=tm, tn=512, residual=x, name=f"w_out_{l}")
        act = _dense(x, w_gu_b[l], tm=tm, tn=512, g=ffn_norm_g[l], swiglu=True, out_dtype=BF16, name=f"w_gu_{l}")
        x = _dense(act, w_down_b[l], tm=tm, tn=256, residual=x, name=f"w_down_{l}")

        outs["pk"].append(po[3].reshape(nb, seq, nh_fx, FX_HD))
        outs["pv"].append(proj[:n_prompt, COL_FX_V : COL_FX_V + fx_w].reshape(nb, seq, nh_fx, FX_HD))
        outs["plf"].append(po[4][:, :nh_fx].reshape(nb, seq, nh_fx))
        outs["prw"].append(rw_state_out(po[5]))
        outs["psh"].append(shift_out(proj[:n_prompt].reshape(nb, seq, COLS_PAD)[:, seq - 1]))
        outs["pgl"].append(gl_state_out(po[6]))
        outs["sk"].append(unpad(so[3]).reshape(ns, dseq, nh_fx, FX_HD))
        outs["sv"].append(proj[n_prompt:rows, COL_FX_V : COL_FX_V + fx_w].reshape(ns, dseq, nh_fx, FX_HD))
        outs["slf"].append(unpad(so[4])[:, :nh_fx].reshape(ns, dseq, nh_fx))
        outs["srw"].append(rw_state_out(so[5]))
        outs["ssh"].append(shift_out(proj[n_prompt:rows].reshape(ns, dseq, COLS_PAD)[:, dseq - 1]))
        outs["sgl"].append(gl_state_out(so[6]))

    st = {k: jnp.stack(v, axis=1) for k, v in outs.items()}
    y_prompt = x[:n_prompt].reshape(nb, seq, d_model)
    y_sample = x[n_prompt:rows].reshape(ns, dseq, d_model)
    return (y_prompt, y_sample, st["pk"], st["pv"], st["plf"], st["prw"], st["psh"], st["pgl"],
            st["sk"], st["sv"], st["slf"], st["srw"], st["ssh"], st["sgl"])
```

```python
import functools

import jax
import jax.numpy as jnp
from jax import lax
from jax.experimental import pallas as pl
from jax.experimental.pallas import tpu as pltpu

F32 = jnp.float32
BF16 = jnp.bfloat16

RW_HD = 64
RW_DECAY_LORA = 32
RW_A_LORA = 32
RW_G_LORA = 96
RW_GN_EPS = 64e-5
FX_HD = 128
GL_HEADS = 4
GL_GATE_LORA = 16
GL_TAU = 16.0
NORM_EPS = 1e-6
PAGE = 128

LANES = 128
VMEM_LIMIT = 56 * 1024 * 1024

NN = (((1,), (0,)), ((), ()))
NT = (((1,), (1,)), ((), ()))
TN = (((0,), (0,)), ((), ()))


def _split2(x):
    hi = x.astype(BF16)
    lo = (x - hi.astype(F32)).astype(BF16)
    return hi, lo


def _split3(x):
    hi = x.astype(BF16)
    r = x - hi.astype(F32)
    mid = r.astype(BF16)
    lo = (r - mid.astype(F32)).astype(BF16)
    return hi, mid, lo


def _mm(a, b, dims=NN, prec=1):
    dot = functools.partial(lax.dot_general, dimension_numbers=dims, preferred_element_type=F32)
    if prec == 1:
        return dot(a.astype(BF16), b.astype(BF16))
    if prec == 3:
        ah, al = _split2(a)
        bh, bl = _split2(b)
        return dot(ah, bh) + (dot(ah, bl) + dot(al, bh))
    if prec == "a":
        bb = b.astype(BF16)
        h, m, l = _split3(a)
        return dot(h, bb) + (dot(m, bb) + dot(l, bb))
    if prec == "b":
        ab = a.astype(BF16)
        h, m, l = _split3(b)
        return dot(ab, h) + (dot(ab, m) + dot(ab, l))
    raise ValueError(prec)


def _sigmoid(x):
    return 1.0 / (1.0 + jnp.exp(-x))


def _softplus(x):
    return jnp.maximum(x, 0.0) + jnp.log1p(jnp.exp(-jnp.abs(x)))


def _iota2(shape, axis):
    return lax.broadcasted_iota(jnp.int32, shape, axis)


def _log2(n):
    assert n & (n - 1) == 0
    return n.bit_length() - 1


def _onehot(mask, dtype=BF16):
    return jnp.where(mask, 1.0, 0.0).astype(dtype)


def _cparams(*sem):
    return pltpu.CompilerParams(dimension_semantics=sem, vmem_limit_bytes=VMEM_LIMIT)


def _dense_kernel(*refs, rms, swiglu, residual, stage):
    it = iter(refs)
    a_ref = next(it)
    g_ref = next(it) if rms else None
    w_ref = next(it)
    w2_ref = next(it) if swiglu else None
    r_ref = next(it) if residual else None
    o_ref = next(it)
    a_sc = next(it) if stage else None

    if stage:
        @pl.when(pl.program_id(1) == 0)
        def _():
            a = a_ref[...].astype(F32)
            if rms:
                a = a * lax.rsqrt(jnp.mean(a * a, axis=-1, keepdims=True) + NORM_EPS) * g_ref[...]
            a_sc[...] = a.astype(BF16)

        a = a_sc[...]
    else:
        a = a_ref[...]
    y = jnp.dot(a, w_ref[...], preferred_element_type=F32)
    if swiglu:
        u = jnp.dot(a, w2_ref[...], preferred_element_type=F32)
        y = y * _sigmoid(y) * u
    if residual:
        y = y + r_ref[...]
    o_ref[...] = y.astype(o_ref.dtype)


def _dense(a, w, *, tm, tn, g=None, residual=None, swiglu=False, out_dtype=F32, name):
    m, k = a.shape
    n = w.shape[1] // 2 if swiglu else w.shape[1]
    assert m % tm == 0 and n % tn == 0
    rms = g is not None
    stage = rms or a.dtype != BF16
    nj = n // tn
    in_specs = [pl.BlockSpec((tm, k), lambda i, j: (i, 0))]
    args = [a]
    if rms:
        in_specs.append(pl.BlockSpec((1, k), lambda i, j: (0, 0)))
        args.append(g.reshape(1, k))
    in_specs.append(pl.BlockSpec((k, tn), lambda i, j: (0, j)))
    args.append(w)
    if swiglu:
        in_specs.append(pl.BlockSpec((k, tn), lambda i, j: (0, j + nj)))
        args.append(w)
    if residual is not None:
        in_specs.append(pl.BlockSpec((tm, tn), lambda i, j: (i, j)))
        args.append(residual)
    return pl.pallas_call(
        functools.partial(_dense_kernel, rms=rms, swiglu=swiglu, residual=residual is not None, stage=stage),
        grid=(m // tm, nj),
        in_specs=in_specs,
        out_specs=pl.BlockSpec((tm, tn), lambda i, j: (i, j)),
        out_shape=jax.ShapeDtypeStruct((m, n), out_dtype),
        scratch_shapes=[pltpu.VMEM((tm, k), BF16)] if stage else [],
        compiler_params=_cparams("arbitrary", "arbitrary"),
        name=name,
    )(*args)


def _segsum64(x, ones_blk):
    parts = [
        _mm(x[:, LANES * i : LANES * (i + 1)], ones_blk, prec="a") for i in range(x.shape[1] // LANES)
    ]
    return jnp.concatenate(parts, axis=1)


def _head_ones():
    r = _iota2((LANES, LANES), 0) >> _log2(RW_HD)
    c = _iota2((LANES, LANES), 1) >> _log2(RW_HD)
    return _onehot(r == c)


def _rw_prep_kernel(
    pr_ref, pk_ref, pv_ref, pz_ref, sr_ref, sk_ref, sv_ref, sz_ref, vec_ref, muz_ref, lora_ref,
    r_o, k_o, v_o, kk_o, bb_o, lw_o, bonus_o, g_o,
    prev_r, prev_k, prev_v, prev_z,
):
    tb = pr_ref.shape[0]

    @pl.when(pl.program_id(1) == 0)
    def _():
        prev_r[...] = sr_ref[0]
        prev_k[...] = sk_ref[0]
        prev_v[...] = sv_ref[0]
        prev_z[...] = sz_ref[0]

    def shift(p_ref, prev, mu):
        p = p_ref[...]
        row = _iota2(p.shape, 0)
        p_prev = jnp.where(row == 0, prev[...], pltpu.roll(p, 1, 0))
        prev[...] = p[tb - 1 : tb, :]
        return p + mu * (p_prev - p)

    vec = vec_ref[...]
    w0, a0, k_k, k_a, r_k = vec[0:1], vec[1:2], vec[2:3], vec[3:4], vec[4:5]
    r = shift(pr_ref, prev_r, vec[5:6])
    k = shift(pk_ref, prev_k, vec[6:7])
    v = shift(pv_ref, prev_v, vec[7:8])
    z = shift(pz_ref, prev_z, muz_ref[...])

    lane = _iota2(z.shape, 1)
    in_w = lane < RW_DECAY_LORA
    in_a = (lane >= RW_DECAY_LORA) & (lane < RW_DECAY_LORA + RW_A_LORA)
    lora = lora_ref[...]
    lw_lin = _mm(jnp.where(in_w, jnp.tanh(z), 0.0), lora, prec=3)
    a_lin = _mm(jnp.where(in_a, z, 0.0), lora, prec=3)
    g = _mm(jnp.where(in_w | in_a, 0.0, _sigmoid(z)), lora, prec=3)

    lw = -jnp.exp(-_softplus(-(w0 + lw_lin)) - 0.5)
    a = _sigmoid(a0 + a_lin)
    ones_blk = _head_ones()
    kk = k * k_k
    kk = kk / jnp.maximum(jnp.sqrt(_segsum64(kk * kk, ones_blk)), 1e-12)
    k = k * (1.0 + (a - 1.0) * k_a)
    r_o[...] = r
    k_o[...] = k
    v_o[...] = v
    kk_o[...] = kk
    bb_o[...] = a * kk
    lw_o[...] = lw
    bonus_o[...] = _segsum64(r * k * r_k, ones_blk) * v
    g_o[...] = g


def _rw_prep(proj, col0, col_z, shift0, vec, muz, lora, *, nseq, t, tb, name):
    w = 512
    nt = t // tb
    row = lambda b, i: b * nt + i
    sr, sk, sv, sz = shift0
    in_specs = [
        pl.BlockSpec((tb, w), lambda b, i: (row(b, i), col0)),
        pl.BlockSpec((tb, w), lambda b, i: (row(b, i), col0 + 1)),
        pl.BlockSpec((tb, w), lambda b, i: (row(b, i), col0 + 2)),
        pl.BlockSpec((tb, 256), lambda b, i: (row(b, i), col_z)),
        pl.BlockSpec((1, 1, w), lambda b, i: (b, 0, 0)),
        pl.BlockSpec((1, 1, w), lambda b, i: (b, 0, 0)),
        pl.BlockSpec((1, 1, w), lambda b, i: (b, 0, 0)),
        pl.BlockSpec((1, 1, 256), lambda b, i: (b, 0, 0)),
        pl.BlockSpec((8, w), lambda b, i: (0, 0)),
        pl.BlockSpec((1, 256), lambda b, i: (0, 0)),
        pl.BlockSpec((256, w), lambda b, i: (0, 0)),
    ]
    out = jax.ShapeDtypeStruct((nseq * t, w), F32)
    return pl.pallas_call(
        _rw_prep_kernel,
        grid=(nseq, nt),
        in_specs=in_specs,
        out_specs=[pl.BlockSpec((tb, w), lambda b, i: (row(b, i), 0))] * 8,
        out_shape=[out] * 8,
        scratch_shapes=[pltpu.VMEM((1, w), F32)] * 3 + [pltpu.VMEM((1, 256), F32)],
        compiler_params=_cparams("arbitrary", "arbitrary"),
        name=name,
    )(proj, proj, proj, proj, sr, sk, sv, sz, vec, muz, lora)


RW_PREC_SOLVE = 1
RW_PREC_APPLY = 1


def _rw_scan_kernel(
    r_ref, k_ref, v_ref, kk_ref, bb_ref, lw_ref, bonus_ref, g_ref, ln_ref, s0_ref,
    o_ref, st_ref, a_sc, *, c, valid, nchunks,
):
    ci = pl.program_id(1)

    @pl.when(ci == 0)
    def _():
        a_sc[...] = s0_ref[...]

    hp = LANES // RW_HD
    n = hp * c
    ti = _iota2((n, n), 0)
    si = _iota2((n, n), 1)
    same_head = (ti >> _log2(c)) == (si >> _log2(c))
    incl = same_head & (si <= ti)
    strict = same_head & (si < ti)
    tri = _onehot(_iota2((c, c), 1) <= _iota2((c, c), 0))
    ri = _iota2((LANES, LANES), 0)
    cj = _iota2((LANES, LANES), 1)
    eye = ri == cj
    blockdiag = (ri >> _log2(RW_HD)) == (cj >> _log2(RW_HD))
    ones_blk = _onehot(blockdiag)
    own_lane = (_iota2((n, 1), 0) >> _log2(c)) == (_iota2((1, LANES), 1) >> _log2(RW_HD))
    live = None
    if valid < c:
        live = (_iota2((c, 1), 0) + ci * c) < valid
    mms = functools.partial(_mm, prec=RW_PREC_SOLVE)
    mma = functools.partial(_mm, prec=RW_PREC_APPLY)

    def dup(x):
        return jnp.concatenate([x] * hp, axis=0)

    def stack(x):
        return jnp.where(own_lane, dup(x), 0.0)

    def fold(x):
        out = x[0:c]
        for h in range(1, hp):
            out = out + x[h * c : (h + 1) * c]
        return out

    nsb, npair = r_ref.shape[0], r_ref.shape[2] // LANES
    chains = [(s, p) for s in range(nsb) for p in range(npair)]
    tiles = range(len(chains))

    def load(ref, masked):
        out = []
        for s, p in chains:
            x = ref[s, :, LANES * p : LANES * (p + 1)]
            out.append(jnp.where(live, x, 0.0) if (masked and live is not None) else x)
        return out

    r_l, v_l = load(r_ref, False), load(v_ref, False)
    k_l, kk_l, bb_l, lw_l = load(k_ref, True), load(kk_ref, True), load(bb_ref, True), load(lw_ref, True)
    lwc_l = [_mm(tri, lw, prec="b") for lw in lw_l]
    lwe_l = [lwc[c - 1 : c, :] for lwc in lwc_l]
    a_st_l, r_t_l, v_st_l, k_h_l, b_h_l, gram_l = [], [], [], [], [], []
    for p in tiles:
        lwc, lwe = lwc_l[p], lwe_l[p]
        e_neg = jnp.exp(-lwc)
        e_end = jnp.exp(lwe - lwc)
        a_st = stack(kk_l[p] * jnp.exp(lwc - lw_l[p]))
        r_t = r_l[p] * jnp.exp(lwc)
        a_st_l.append(a_st)
        r_t_l.append(r_t)
        v_st_l.append(stack(v_l[p]))
        k_h_l.append(k_l[p] * e_end)
        b_h_l.append(bb_l[p] * e_end)
        gram_l.append(mms(jnp.concatenate([a_st, stack(r_t)], axis=0),
                          jnp.concatenate([dup(bb_l[p] * e_neg), dup(k_l[p] * e_neg)], axis=0), NT))
    l_l = [jnp.where(strict, g[:n, :n], 0.0) for g in gram_l]
    mv_l = [mma(jnp.where(strict, g[:n, n:], 0.0), v_st) for g, v_st in zip(gram_l, v_st_l)]
    t_l = [_onehot(ti == si, F32) for _ in tiles]
    lev = 0
    while (1 << lev) < c:
        off = ((ti >> (lev + 1)) == (si >> (lev + 1))) & ((ti >> lev) != (si >> lev))
        tl_l = [mms(t, jnp.where(off, l, 0.0)) for t, l in zip(t_l, l_l)]
        t_l = [t - mms(tl, t) for t, tl in zip(t_l, tl_l)]
        lev += 1
    z_l = [mma(t, jnp.concatenate([a_st, mv], axis=1)) for t, a_st, mv in zip(t_l, a_st_l, mv_l)]
    qz_l = [mma(jnp.where(incl, g[n:, :n], 0.0), z) for g, z in zip(gram_l, z_l)]
    pv_l = [mma(jnp.where(incl, g[n:, n:], 0.0), v_st) for g, v_st in zip(gram_l, v_st_l)]
    bz_l = [mma(b_h, jnp.concatenate([fold(z[:, :LANES]), fold(z[:, LANES:])], axis=1), TN)
            for b_h, z in zip(b_h_l, z_l)]
    kv_l = [mma(k_h, v, TN) for k_h, v in zip(k_h_l, v_l)]
    ya_l = []
    for p in tiles:
        ry = r_t_l[p] - fold(qz_l[p][:, :LANES])
        g_mat = jnp.where(eye, jnp.exp(lwe_l[p]), 0.0) - jnp.where(blockdiag, bz_l[p][:, :LANES], 0.0)
        ya_l.append(mma(jnp.concatenate([ry, g_mat], axis=0), a_sc[chains[p]]))
    ys = [ya[:c] + fold(pv - qz[:, LANES:]) for ya, pv, qz in zip(ya_l, pv_l, qz_l)]
    states = [ya[c:] + jnp.where(blockdiag, kv - bz[:, LANES:], 0.0) for ya, kv, bz in zip(ya_l, kv_l, bz_l)]
    a_sc[...] = jnp.stack(states).reshape(a_sc.shape)
    y = jnp.concatenate([jnp.concatenate(ys[s * npair : (s + 1) * npair], axis=1) for s in range(nsb)], axis=0)
    mean = _segsum64(y, ones_blk) * (1.0 / RW_HD)
    d = y - mean
    var = _segsum64(d * d, ones_blk) * (1.0 / RW_HD)
    yn = (d * lax.rsqrt(var + RW_GN_EPS)).reshape(o_ref.shape)
    o_ref[...] = (yn * ln_ref[0:1, :] + ln_ref[1:2, :] + bonus_ref[...]) * g_ref[...]

    @pl.when(ci == nchunks - 1)
    def _():
        st_ref[...] = a_sc[...]


def _rw_scan(prep, ln, s0, *, nseq, t, c, valid, name):
    w = 512
    nchunks = -(-valid // c)
    npair = w // LANES
    nsb = RW_SEQS_PER_STEP if nseq % RW_SEQS_PER_STEP == 0 else 1
    blk = pl.BlockSpec((nsb, c, w), lambda b, i: (b, i, 0))
    st_spec = pl.BlockSpec((nsb, npair, LANES, LANES), lambda b, i: (b, 0, 0, 0))
    o, st = pl.pallas_call(
        functools.partial(_rw_scan_kernel, c=c, valid=valid, nchunks=nchunks),
        grid=(nseq // nsb, nchunks),
        in_specs=[blk] * 8 + [pl.BlockSpec((2, w), lambda b, i: (0, 0)), st_spec],
        out_specs=[blk, st_spec],
        out_shape=[
            jax.ShapeDtypeStruct((nseq, nchunks * c, w), F32),
            jax.ShapeDtypeStruct((nseq, npair, LANES, LANES), F32),
        ],
        scratch_shapes=[pltpu.VMEM((nsb, npair, LANES, LANES), F32)],
        compiler_params=_cparams("arbitrary", "arbitrary"),
        name=name,
    )(*[a.reshape(nseq, t, w) for a in prep], ln, s0)
    return o.reshape(nseq * nchunks * c, w), st


GL_PREC = 3


def _gla_kernel(*refs, tb, valid, nblocks, dk, nsb):
    seq_refs = [refs[5 * s : 5 * s + 5] for s in range(nsb)]
    ab_ref, abias_ref, ng_ref, s0_ref, o_ref, st_ref, s_sc = refs[5 * nsb :]
    bi = pl.program_id(1)

    @pl.when(bi == 0)
    def _():
        s_sc[...] = s0_ref[...]

    mm = functools.partial(_mm, prec=GL_PREC)
    ti = _iota2((tb, tb), 0)
    si = _iota2((tb, tb), 1)
    tri = _onehot(si <= ti)
    trow = _iota2((tb, 1), 0)
    lane_head = _iota2((1, LANES), 1) >> _log2(dk)
    heads_per_tile = LANES // dk
    own_lane = (_iota2((heads_per_tile * tb, 1), 0) >> _log2(tb)) == lane_head

    def dup(x):
        return jnp.concatenate([x] * heads_per_tile, axis=0)

    def stack(x):
        return jnp.where(own_lane, dup(x), 0.0)

    npair = seq_refs[0][0].shape[1] // LANES
    chains = [(s, p) for s in range(nsb) for p in range(npair)]
    q_l, k_l, la_l = [], [], []
    for s in range(nsb):
        q_ref, k_ref, _, xa_ref, _ = seq_refs[s]
        la_all = -_softplus(-(_mm(xa_ref[...], ab_ref[...], prec=3) + abias_ref[...])) * (1.0 / GL_TAU)
        k_all = k_ref[...]
        if valid < tb:
            live = (trow + bi * tb) < valid
            la_all = jnp.where(live, la_all, 0.0)
            k_all = jnp.where(live, k_all, 0.0)
        for p in range(npair):
            sl = slice(LANES * p, LANES * (p + 1))
            q_l.append(q_ref[:, sl] * (dk**-0.5))
            k_l.append(k_all[:, sl])
            la_l.append(la_all[:, sl])
    b_l = [_mm(tri, la, prec="b") for la in la_l]
    att_l = [jnp.where(dup(ti == si), mm(stack(q), k, NT), 0.0) for q, k in zip(q_l, k_l)]
    lev = 1
    while (1 << lev) <= tb:
        half = 1 << (lev - 1)
        sel = _onehot(si == ((ti >> lev) << lev) + (half - 1))
        upper = ((trow >> (lev - 1)) & 1) == 1
        same = dup((ti >> lev) == (si >> lev))
        bref_l = [_mm(sel, b, prec="b") for b in b_l]
        qs_l = [stack(jnp.where(upper, q * jnp.exp(jnp.minimum(b - br, 0.0)), 0.0)) for q, b, br in zip(q_l, b_l, bref_l)]
        ks_l = [jnp.where(upper, 0.0, k * jnp.exp(jnp.minimum(br - b, 0.0))) for k, b, br in zip(k_l, b_l, bref_l)]
        att_l = [att + jnp.where(same, mm(qs, ks, NT), 0.0) for att, qs, ks in zip(att_l, qs_l, ks_l)]
        lev += 1

    s_old_l = [s_sc[s, p] for s, p in chains]
    inter_l = [mm(stack(q * jnp.exp(b)), s_old, NT) for q, b, s_old in zip(q_l, b_l, s_old_l)]
    outs, states = [], []
    for i, (s, p) in enumerate(chains):
        b_end = b_l[i][tb - 1 : tb, :]
        ke = k_l[i] * jnp.exp(b_end - b_l[i])
        s_new = s_old_l[i] * jnp.exp(b_end)
        for h in range(heads_per_tile):
            hv = heads_per_tile * p + h
            v = seq_refs[s][2][:, LANES * hv : LANES * (hv + 1)]
            o = mm(att_l[i][h * tb : (h + 1) * tb], v) + inter_l[i][h * tb : (h + 1) * tb]
            s_new = s_new + mm(v, jnp.where(lane_head == h, ke, 0.0), TN)
            outs.append(o * lax.rsqrt(jnp.mean(o * o, axis=-1, keepdims=True) + NORM_EPS) * ng_ref[...])
        states.append(s_new)

    s_sc[...] = jnp.stack(states).reshape(s_sc.shape)
    per_seq = len(outs) // nsb
    for s in range(nsb):
        gg = seq_refs[s][4][...]
        o_ref[s] = jnp.concatenate(outs[s * per_seq : (s + 1) * per_seq], axis=1) * (gg * _sigmoid(gg))

    @pl.when(bi == nblocks - 1)
    def _():
        st_ref[...] = s_sc[...]


def _gla(proj, cols, ab, abias, ng, s0, *, nseq, t, tb, valid, name):
    cq, ck, cv, cxa, cgg = cols
    nb = t // tb
    nrun = -(-valid // tb)
    npair = 256 // LANES
    nsb = GL_SEQS_PER_STEP if nseq % GL_SEQS_PER_STEP == 0 else 1
    st_spec = pl.BlockSpec((nsb, npair, LANES, LANES), lambda b, i: (b, 0, 0, 0))

    def seq_specs(s):
        row = lambda b, i: (b * nsb + s) * nb + i
        return [
            pl.BlockSpec((tb, 256), lambda b, i: (row(b, i), cq)),
            pl.BlockSpec((tb, 256), lambda b, i: (row(b, i), ck)),
            pl.BlockSpec((tb, 512), lambda b, i: (row(b, i), cv)),
            pl.BlockSpec((tb, LANES), lambda b, i: (row(b, i), cxa)),
            pl.BlockSpec((tb, 512), lambda b, i: (row(b, i), cgg)),
        ]

    o, st = pl.pallas_call(
        functools.partial(_gla_kernel, tb=tb, valid=valid, nblocks=nrun, dk=256 // GL_HEADS, nsb=nsb),
        grid=(nseq // nsb, nrun),
        in_specs=[spec for s in range(nsb) for spec in seq_specs(s)]
        + [
            pl.BlockSpec((LANES, 256), lambda b, i: (0, 0)),
            pl.BlockSpec((1, 256), lambda b, i: (0, 0)),
            pl.BlockSpec((1, LANES), lambda b, i: (0, 0)),
            st_spec,
        ],
        out_specs=[pl.BlockSpec((nsb, tb, 512), lambda b, i: (b, i, 0)), st_spec],
        out_shape=[
            jax.ShapeDtypeStruct((nseq, nrun * tb, 512), F32),
            jax.ShapeDtypeStruct((nseq, npair, LANES, LANES), F32),
        ],
        scratch_shapes=[pltpu.VMEM((nsb, npair, LANES, LANES), F32)],
        compiler_params=_cparams("arbitrary", "arbitrary"),
        name=name,
    )(*([proj] * (5 * nsb)), ab, abias, ng, s0)
    return o.reshape(nseq * nrun * tb, 512), st


def _fx_prep_kernel(q_ref, k_ref, v_ref, fl_ref, qg_ref, kg_ref, bf_ref,
                    qb_o, kf_o, kb_o, vb_o, lf_o, c_o, carry):
    tb = q_ref.shape[0]

    @pl.when(pl.program_id(1) == 0)
    def _():
        carry[...] = jnp.zeros_like(carry)

    def headnorm(x, gain):
        return x * lax.rsqrt(jnp.mean(x * x, axis=-1, keepdims=True) + NORM_EPS) * gain

    for h in range(q_ref.shape[1] // FX_HD):
        sl = slice(FX_HD * h, FX_HD * (h + 1))
        qb_o[:, sl] = (headnorm(q_ref[:, sl], qg_ref[...]) * (FX_HD**-0.5)).astype(BF16)
        kn = headnorm(k_ref[:, sl], kg_ref[...])
        kf_o[:, sl] = kn
        kb_o[:, sl] = kn.astype(BF16)
    vb_o[...] = v_ref[...].astype(BF16)
    logf = -_softplus(-(fl_ref[...] + bf_ref[...]))
    lf_o[...] = logf
    tri = _onehot(_iota2((tb, tb), 1) <= _iota2((tb, tb), 0))
    c = _mm(tri, logf, prec="b") + carry[...]
    c_o[...] = c
    carry[...] = c[tb - 1 : tb, :]


def _fx_prep(proj, cols, qg, kg, bf, *, nseq, t, tb, name):
    cq, ck, cv, cfl = cols
    nt = t // tb
    row = lambda b, i: b * nt + i
    w = 1024
    wide = pl.BlockSpec((tb, w), lambda b, i: (row(b, i), 0))
    thin = pl.BlockSpec((tb, LANES), lambda b, i: (row(b, i), 0))
    vecspec = pl.BlockSpec((1, LANES), lambda b, i: (0, 0))
    rows = nseq * t
    return pl.pallas_call(
        _fx_prep_kernel,
        grid=(nseq, nt),
        in_specs=[
            pl.BlockSpec((tb, w), lambda b, i: (row(b, i), cq)),
            pl.BlockSpec((tb, w), lambda b, i: (row(b, i), ck)),
            pl.BlockSpec((tb, w), lambda b, i: (row(b, i), cv)),
            pl.BlockSpec((tb, LANES), lambda b, i: (row(b, i), cfl)),
            vecspec, vecspec, vecspec,
        ],
        out_specs=[wide, wide, wide, wide, thin, thin],
        out_shape=[
            jax.ShapeDtypeStruct((rows, w), BF16),
            jax.ShapeDtypeStruct((rows, w), F32),
            jax.ShapeDtypeStruct((rows, w), BF16),
            jax.ShapeDtypeStruct((rows, w), BF16),
            jax.ShapeDtypeStruct((rows, LANES), F32),
            jax.ShapeDtypeStruct((rows, LANES), F32),
        ],
        scratch_shapes=[pltpu.VMEM((1, LANES), F32)],
        compiler_params=_cparams("arbitrary", "arbitrary"),
        name=name,
    )(proj, proj, proj, proj, qg, kg, bf)


def _flash_kernel(q_ref, k_ref, v_ref, c_ref, cref_ref, o_ref, m_sc, l_sc, acc_sc, *, tq, tk, hb):
    qi = pl.program_id(2)
    kj = pl.program_id(3)

    @pl.when(kj == 0)
    def _():
        m_sc[...] = jnp.full_like(m_sc, -jnp.inf)
        l_sc[...] = jnp.zeros_like(l_sc)
        acc_sc[...] = jnp.zeros_like(acc_sc)

    def update(masked):
        heads = range(hb)
        sl = [slice(FX_HD * h, FX_HD * (h + 1)) for h in heads]
        s_l = [lax.dot_general(q_ref[:, sl[h]], k_ref[:, sl[h]], NT, preferred_element_type=F32) for h in heads]
        s_l = [s_l[h] + (cref_ref[0, h, :, 0:1] - c_ref[0, h]) for h in heads]
        if masked:
            keep = (kj * tk + _iota2((tq, tk), 1)) <= (qi * tq + _iota2((tq, tk), 0))
            s_l = [jnp.where(keep, s, -jnp.inf) for s in s_l]
        m_old = [m_sc[h] for h in heads]
        m_new = [jnp.maximum(m_old[h], jnp.max(s_l[h], axis=-1, keepdims=True)) for h in heads]
        p_l = [jnp.exp(s_l[h] - m_new[h]) for h in heads]
        pv_l = [jnp.dot(p_l[h].astype(BF16), v_ref[:, sl[h]], preferred_element_type=F32) for h in heads]
        for h in heads:
            alpha = jnp.exp(m_old[h] - m_new[h])
            l_sc[h] = alpha * l_sc[h] + jnp.sum(p_l[h], axis=-1, keepdims=True)
            acc_sc[h] = alpha * acc_sc[h] + pv_l[h]
            m_sc[h] = m_new[h]

    first_key, last_key = kj * tk, kj * tk + (tk - 1)
    first_query, last_query = qi * tq, qi * tq + (tq - 1)

    @pl.when(last_key <= first_query)
    def _():
        update(False)

    @pl.when((last_key > first_query) & (first_key <= last_query))
    def _():
        update(True)

    @pl.when(kj == pl.num_programs(3) - 1)
    def _():
        for h in range(hb):
            o_ref[:, FX_HD * h : FX_HD * (h + 1)] = acc_sc[h] / l_sc[h]


def _flash(qb, kb, vb, c_row, *, nseq, t, tq, tk, name):
    nh = qb.shape[1] // FX_HD
    hb = FLASH_HEADS_PER_STEP if nh % FLASH_HEADS_PER_STEP == 0 else 1
    nq, nk = t // tq, t // tk
    last = lambda i: (i * tq + tq - 1) // tk
    w = hb * FX_HD
    return pl.pallas_call(
        functools.partial(_flash_kernel, tq=tq, tk=tk, hb=hb),
        grid=(nseq, nh // hb, nq, nk),
        in_specs=[
            pl.BlockSpec((tq, w), lambda b, h, i, j: (b * nq + i, h)),
            pl.BlockSpec((tk, w), lambda b, h, i, j: (b * nk + jnp.minimum(j, last(i)), h)),
            pl.BlockSpec((tk, w), lambda b, h, i, j: (b * nk + jnp.minimum(j, last(i)), h)),
            pl.BlockSpec((1, hb, 1, tk), lambda b, h, i, j: (b, h, 0, jnp.minimum(j, last(i)))),
            pl.BlockSpec((1, hb, 1, LANES), lambda b, h, i, j: (b, h, 0, i * (tq // LANES))),
        ],
        out_specs=pl.BlockSpec((tq, w), lambda b, h, i, j: (b * nq + i, h)),
        out_shape=jax.ShapeDtypeStruct((nseq * t, nh * FX_HD), F32),
        scratch_shapes=[
            pltpu.VMEM((hb, tq, 1), F32),
            pltpu.VMEM((hb, tq, 1), F32),
            pltpu.VMEM((hb, tq, FX_HD), F32),
        ],
        compiler_params=_cparams("arbitrary", "arbitrary", "arbitrary", "arbitrary"),
        name=name,
    )(qb, kb, vb, c_row, c_row)


def _decode_kernel(pt_ref, q_ref, kn_ref, vn_ref, cn_ref, *rest, nq, nh, pp):
    kc_refs, vc_refs, lf_refs = rest[0:pp], rest[pp : 2 * pp], rest[2 * pp : 3 * pp]
    o_ref, m_sc, l_sc, acc_sc, carry = rest[3 * pp :]
    j = pl.program_id(1)
    rows, cols = nh * nq, PAGE * nh
    row = _iota2((rows, cols), 0)
    col = _iota2((rows, cols), 1)
    own = (row >> _log2(nq)) == (col & (nh - 1))

    def attend(pages):
        scores = [
            jnp.where(keep, lax.dot_general(q_ref[...], k_rows, NT, preferred_element_type=F32) + bias_row, -jnp.inf)
            for k_rows, _, bias_row, keep in pages
        ]
        m_old = m_sc[...]
        m_new = m_old
        for s in scores:
            m_new = jnp.maximum(m_new, jnp.max(s, axis=-1, keepdims=True))
        alpha = jnp.exp(m_old - m_new)
        l_new = alpha * l_sc[...]
        acc = alpha * acc_sc[...]
        for s, (_, v_rows, _, _) in zip(scores, pages):
            p = jnp.exp(s - m_new)
            l_new = l_new + jnp.sum(p, axis=-1, keepdims=True)
            acc = acc + jnp.dot(p.astype(BF16), v_rows, preferred_element_type=F32)
        l_sc[...] = l_new
        acc_sc[...] = acc
        m_sc[...] = m_new

    @pl.when(j == 0)
    def _():
        m_sc[...] = jnp.full_like(m_sc, -jnp.inf)
        l_sc[...] = jnp.zeros_like(l_sc)
        acc_sc[...] = jnp.zeros_like(acc_sc)
        carry[...] = jnp.zeros_like(carry)
        causal = own & ((col >> _log2(nh)) <= (row & (nq - 1)))
        attend([(kn_ref[...], vn_ref[...], -cn_ref[0], causal)])

    @pl.when(j > 0)
    def _():
        spread = _onehot((_iota2((nh, cols), 1) & (nh - 1)) == _iota2((nh, cols), 0))
        after = _iota2((PAGE, cols), 0) > (_iota2((PAGE, cols), 1) >> _log2(nh))
        run = carry[...]
        pages = []
        for i in range(pp):
            x = _mm(lf_refs[i][...], spread, prec="a")
            bias_row = run + jnp.sum(jnp.where(after, x, 0.0), axis=0, keepdims=True)
            run = run + jnp.sum(x, axis=0, keepdims=True)
            pages.append((kc_refs[i][...].astype(BF16), vc_refs[i][...].astype(BF16), bias_row, own))
        carry[...] = run
        attend(pages)

    @pl.when(j == pl.num_programs(1) - 1)
    def _():
        o_ref[...] = acc_sc[...] / l_sc[...]


def _decode(q_hq, kn_rows, vn_rows, cn_row, cache_k, cache_v, cache_logf, page_table, layer, *, nq, name):
    nseq, npages = page_table.shape
    pool, depth, _, nh, hd = cache_k.shape
    pp = DEC_PAGES_PER_STEP if npages % DEC_PAGES_PER_STEP == 0 else 1
    rows, cols = nh * nq, PAGE * nh
    ck = cache_k.reshape(pool, depth, cols, hd)
    cv = cache_v.reshape(pool, depth, cols, hd)

    def page_of(i):
        def index(b, j, pt):
            back = (jnp.maximum(j, 1) - 1) * pp + i
            return (pt[b * npages + (npages - 1 - back)], layer, 0, 0)

        return index

    per_seq = lambda b, j, pt: (b, 0)
    grid_spec = pltpu.PrefetchScalarGridSpec(
        num_scalar_prefetch=1,
        grid=(nseq, 1 + npages // pp),
        in_specs=[
            pl.BlockSpec((rows, hd), per_seq),
            pl.BlockSpec((cols, hd), per_seq),
            pl.BlockSpec((cols, hd), per_seq),
            pl.BlockSpec((1, 1, cols), lambda b, j, pt: (b, 0, 0)),
        ]
        + [pl.BlockSpec((None, None, cols, hd), page_of(i)) for i in range(pp)] * 2
        + [pl.BlockSpec((None, None, PAGE, nh), page_of(i)) for i in range(pp)],
        out_specs=pl.BlockSpec((rows, hd), per_seq),
        scratch_shapes=[
            pltpu.VMEM((rows, 1), F32),
            pltpu.VMEM((rows, 1), F32),
            pltpu.VMEM((rows, hd), F32),
            pltpu.VMEM((1, cols), F32),
        ],
    )
    return pl.pallas_call(
        functools.partial(_decode_kernel, nq=nq, nh=nh, pp=pp),
        grid_spec=grid_spec,
        out_shape=jax.ShapeDtypeStruct((nseq * rows, hd), F32),
        compiler_params=_cparams("arbitrary", "arbitrary"),
        name=name,
    )(page_table.reshape(-1), q_hq, kn_rows, vn_rows, cn_row, *([ck] * pp), *([cv] * pp), *([cache_logf] * pp))


RW_CHUNK = 64
RW_SEQS_PER_STEP = 2
GL_BLOCK = 128
GL_SEQS_PER_STEP = 2
FLASH_HEADS_PER_STEP = 2
DEC_Q_ROWS = 16
DEC_PAGES_PER_STEP = 4
TM = 768
COL_FX_Q, COL_FX_K, COL_FX_V = 0, 1024, 2048
COL_RW_R = 3072
COL_GL_V, COL_GL_GG = 4608, 5120
COL_GL_Q, COL_GL_K = 5632, 5888
COL_RW_Z = 6144
COL_FX_FL = 6400
COL_GL_XA = 6528
COLS_PAD = 6656


def _pad_cols(x, width):
    return jnp.pad(x, [(0, 0)] * (x.ndim - 1) + [(0, width - x.shape[-1])])


def _relayout_in(w, rw_w, fx_w, gl_kw, gl_vw, nh_fx):
    rw_z = RW_DECAY_LORA + RW_A_LORA + RW_G_LORA
    o = 0
    rw_rkv = w[..., o : o + 3 * rw_w]; o += 3 * rw_w
    rw_zz = w[..., o : o + rw_z]; o += rw_z
    fx_qkv = w[..., o : o + 3 * fx_w]; o += 3 * fx_w
    fx_fl = w[..., o : o + nh_fx]; o += nh_fx
    gl_qk = w[..., o : o + 2 * gl_kw]; o += 2 * gl_kw
    gl_v = w[..., o : o + gl_vw]; o += gl_vw
    gl_xa = w[..., o : o + GL_GATE_LORA]; o += GL_GATE_LORA
    gl_gg = w[..., o : o + gl_vw]; o += gl_vw
    assert o == w.shape[-1]
    return jnp.concatenate(
        [fx_qkv, rw_rkv, gl_v, gl_gg, gl_qk, _pad_cols(rw_zz, 256), _pad_cols(fx_fl, LANES), _pad_cols(gl_xa, LANES)],
        axis=-1,
    )


def _mixers(proj, layer, lp, states, paged, *, nseq, t, valid, tag):
    (shift0, rw_s0, gl_s0) = states
    tb = min(t, 256)
    prep = _rw_prep(proj, COL_RW_R // 512, COL_RW_Z // 256, shift0, lp["rw_vec"], lp["rw_muz"], lp["rw_lora"],
                    nseq=nseq, t=t, tb=tb, name=f"rw_prep_{tag}")
    o_rw, rw_state = _rw_scan(prep, lp["rw_ln"], rw_s0, nseq=nseq, t=t, c=min(t, RW_CHUNK), valid=valid,
                              name=f"rw_scan_{tag}")
    o_gl, gl_state = _gla(
        proj, (COL_GL_Q // 256, COL_GL_K // 256, COL_GL_V // 512, COL_GL_XA // LANES, COL_GL_GG // 512),
        lp["gl_ab"], lp["gl_abias"], lp["gl_ng"], gl_s0, nseq=nseq, t=t, tb=min(t, GL_BLOCK), valid=valid,
        name=f"gla_{tag}")
    qb, kf, kb, vb, logf, c = _fx_prep(
        proj, (COL_FX_Q // 1024, COL_FX_K // 1024, COL_FX_V // 1024, COL_FX_FL // LANES),
        lp["fx_qg"], lp["fx_kg"], lp["fx_bf"], nseq=nseq, t=t, tb=tb, name=f"fx_prep_{tag}")
    nh = kf.shape[1] // FX_HD
    if paged is None:
        c_t = jnp.transpose(c.reshape(nseq, t, LANES)[:, :, :nh], (0, 2, 1))
        blk = min(t, 512)
        o_fx = _flash(qb, kb, vb, c_t[:, :, None, :], nseq=nseq, t=t, tq=blk, tk=blk, name=f"flash_{tag}")
    else:
        cache_k, cache_v, cache_logf, page_table = paged
        nq = DEC_Q_ROWS
        assert t == PAGE and valid <= nq
        q_hq = jnp.transpose(qb.reshape(nseq, t, nh, FX_HD)[:, :nq], (0, 2, 1, 3)).reshape(nseq * nh * nq, FX_HD)
        o_hq = _decode(q_hq, kb.reshape(nseq * t * nh, FX_HD), vb.reshape(nseq * t * nh, FX_HD),
                       c[:, :nh].reshape(nseq, 1, t * nh), cache_k, cache_v, cache_logf, page_table, layer,
                       nq=nq, name=f"decode_{tag}")
        o_fx = jnp.transpose(o_hq.reshape(nseq, nh, nq, FX_HD)[:, :, :valid], (0, 2, 1, 3)).reshape(nseq * valid, -1)
    return o_rw, o_fx, o_gl, kf, logf, rw_state, gl_state


def kernel(x_prompt, x_sample, cache_k, cache_v, cache_logf, state_rwkv, state_rwkv_shift, state_gla,
           page_table, attn_norm_g, w_in, rw_mu, rw_w_b, rw_w0, rw_a_b, rw_a0, rw_g_b, rw_k_k, rw_k_a,
           rw_r_k, rw_lnx_g, rw_lnx_b, fx_qn_g, fx_kn_g, fx_b_f, gl_alpha_b, gl_alpha_bias, gl_norm_g,
           w_out, ffn_norm_g, w_gu, w_down):
    nb, seq, d_model = x_prompt.shape
    ns, dseq, _ = x_sample.shape
    depth = w_in.shape[0]
    rw_w = rw_w0.shape[1]
    rw_heads = rw_w // RW_HD
    nh_fx = fx_b_f.shape[1]
    fx_w = nh_fx * FX_HD
    gl_kw = gl_alpha_b.shape[2]
    gl_dk = gl_kw // GL_HEADS
    gl_dv = gl_norm_g.shape[1]
    gl_vw = GL_HEADS * gl_dv
    rw_cols = 3 * rw_w + RW_DECAY_LORA + RW_A_LORA + RW_G_LORA
    assert (rw_w, fx_w, gl_kw, gl_vw, FX_HD, gl_dv) == (512, 1024, 256, 512, 128, 128)
    n_prompt = nb * seq
    n_sample = ns * dseq
    rows = n_prompt + n_sample
    tm = TM if rows >= TM else 64
    rows_pad = -(-rows // tm) * tm
    relayout = functools.partial(_relayout_in, rw_w=rw_w, fx_w=fx_w, gl_kw=gl_kw, gl_vw=gl_vw, nh_fx=nh_fx)

    w_in_p = relayout(w_in).astype(BF16)
    w_out_b = w_out.astype(BF16)
    w_gu_b = w_gu.astype(BF16)
    w_down_b = w_down.astype(BF16)
    d_ff = w_down.shape[1]
    rw_mu_r, rw_mu_k, rw_mu_v, rw_mu_z = (
        rw_mu[:, :rw_w], rw_mu[:, rw_w : 2 * rw_w], rw_mu[:, 2 * rw_w : 3 * rw_w], rw_mu[:, 3 * rw_w :])
    layers = []
    for l in range(depth):
        layers.append(dict(
            rw_vec=jnp.stack([rw_w0[l], rw_a0[l], rw_k_k[l], rw_k_a[l], rw_r_k[l].reshape(-1),
                              rw_mu_r[l], rw_mu_k[l], rw_mu_v[l]]),
            rw_muz=_pad_cols(rw_mu_z[l][None], 256),
            rw_lora=jnp.pad(jnp.concatenate([rw_w_b[l], rw_a_b[l], rw_g_b[l]], axis=0),
                            ((0, 256 - (rw_cols - 3 * rw_w)), (0, 0))),
            rw_ln=jnp.stack([rw_lnx_g[l], rw_lnx_b[l]]),
            fx_qg=fx_qn_g[l][None], fx_kg=fx_kn_g[l][None], fx_bf=_pad_cols(fx_b_f[l][None], LANES),
            gl_ab=jnp.pad(gl_alpha_b[l], ((0, LANES - GL_GATE_LORA), (0, 0))),
            gl_abias=gl_alpha_bias[l][None], gl_ng=gl_norm_g[l][None],
        ))

    def rw_state_in(s):
        n = s.shape[0]
        a = jnp.swapaxes(s, -1, -2).reshape(n, rw_heads // 2, 2, RW_HD, RW_HD)
        z = jnp.zeros_like(a[:, :, 0])
        top = jnp.concatenate([a[:, :, 0], z], axis=-1)
        bot = jnp.concatenate([z, a[:, :, 1]], axis=-1)
        return jnp.concatenate([top, bot], axis=-2)

    def rw_state_out(a):
        n = a.shape[0]
        h0 = a[:, :, :RW_HD, :RW_HD]
        h1 = a[:, :, RW_HD:, RW_HD:]
        return jnp.swapaxes(jnp.stack([h0, h1], axis=2).reshape(n, rw_heads, RW_HD, RW_HD), -1, -2)

    def gl_state_in(s):
        n = s.shape[0]
        a = jnp.swapaxes(s, -1, -2).reshape(n, GL_HEADS // 2, 2, gl_dv, gl_dk)
        return jnp.transpose(a, (0, 1, 3, 2, 4)).reshape(n, GL_HEADS // 2, gl_dv, 2 * gl_dk)

    def gl_state_out(a):
        n = a.shape[0]
        a = jnp.transpose(a.reshape(n, GL_HEADS // 2, gl_dv, 2, gl_dk), (0, 1, 3, 2, 4))
        return jnp.swapaxes(a.reshape(n, GL_HEADS, gl_dv, gl_dk), -1, -2)

    def shift_in(s):
        return (s[:, None, :rw_w], s[:, None, rw_w : 2 * rw_w], s[:, None, 2 * rw_w : 3 * rw_w],
                _pad_cols(s[:, None, 3 * rw_w :], 256))

    def shift_out(proj_rows):
        return jnp.concatenate([proj_rows[:, COL_RW_R : COL_RW_R + 3 * rw_w],
                                proj_rows[:, COL_RW_Z : COL_RW_Z + rw_cols - 3 * rw_w]], axis=-1)

    x = jnp.concatenate([x_prompt.reshape(n_prompt, d_model), x_sample.reshape(n_sample, d_model),
                         jnp.zeros((rows_pad - rows, d_model), x_prompt.dtype)], axis=0)
    t_s = PAGE
    p_states = (shift_in(jnp.zeros((nb, rw_cols), F32)), jnp.zeros((nb, rw_heads // 2, LANES, LANES), F32),
                jnp.zeros((nb, GL_HEADS // 2, LANES, LANES), F32))
    outs = {k: [] for k in ("pk", "pv", "plf", "prw", "psh", "pgl", "sk", "sv", "slf", "srw", "ssh", "sgl")}
    for l in range(depth):
        lp = layers[l]
        proj = _dense(x, w_in_p[l], tm=tm, tn=512, g=attn_norm_g[l], name=f"w_in_{l}")
        proj_s = jnp.pad(proj[n_prompt:rows].reshape(ns, dseq, COLS_PAD), ((0, 0), (0, t_s - dseq), (0, 0)))
        proj_s = proj_s.reshape(ns * t_s, COLS_PAD)
        s_states = (shift_in(state_rwkv_shift[:, l]), rw_state_in(state_rwkv[:, l]), gl_state_in(state_gla[:, l]))
        po = _mixers(proj, l, lp, p_states, None, nseq=nb, t=seq, valid=seq, tag=f"p{l}")
        so = _mixers(proj_s, l, lp, s_states, (cache_k, cache_v, cache_logf, page_table),
                     nseq=ns, t=t_s, valid=dseq, tag=f"s{l}")

        def unpad(a):
            return a.reshape(ns, a.shape[0] // ns, -1)[:, :dseq].reshape(n_sample, -1)

        mix = jnp.concatenate([
            jnp.concatenate([po[0], po[1], po[2]], axis=1),
            jnp.concatenate([unpad(so[0]), so[1], unpad(so[2])], axis=1),
            jnp.zeros((rows_pad - rows, d_model), F32)], axis=0)
        x = _dense(mix, w_out_b[l], tm=tm, tn=512, residual=x, name=f"w_out_{l}")
        act = _dense(x, w_gu_b[l], tm=tm, tn=512, g=ffn_norm_g[l], swiglu=True, out_dtype=BF16, name=f"w_gu_{l}")
        x = _dense(act, w_down_b[l], tm=tm, tn=256, residual=x, name=f"w_down_{l}")

        outs["pk"].append(po[3].reshape(nb, seq, nh_fx, FX_HD))
        outs["pv"].append(proj[:n_prompt, COL_FX_V : COL_FX_V + fx_w].reshape(nb, seq, nh_fx, FX_HD))
        outs["plf"].append(po[4][:, :nh_fx].reshape(nb, seq, nh_fx))
        outs["prw"].append(rw_state_out(po[5]))
        outs["psh"].append(shift_out(proj[:n_prompt].reshape(nb, seq, COLS_PAD)[:, seq - 1]))
        outs["pgl"].append(gl_state_out(po[6]))
        outs["sk"].append(unpad(so[3]).reshape(ns, dseq, nh_fx, FX_HD))
        outs["sv"].append(proj[n_prompt:rows, COL_FX_V : COL_FX_V + fx_w].reshape(ns, dseq, nh_fx, FX_HD))
        outs["slf"].append(unpad(so[4])[:, :nh_fx].reshape(ns, dseq, nh_fx))
        outs["srw"].append(rw_state_out(so[5]))
        outs["ssh"].append(shift_out(proj[n_prompt:rows].reshape(ns, dseq, COLS_PAD)[:, dseq - 1]))
        outs["sgl"].append(gl_state_out(so[6]))

    st = {k: jnp.stack(v, axis=1) for k, v in outs.items()}
    y_prompt = x[:n_prompt].reshape(nb, seq, d_model)
    y_sample = x[n_prompt:rows].reshape(ns, dseq, d_model)
    return (y_prompt, y_sample, st["pk"], st["pv"], st["plf"], st["prw"], st["psh"], st["pgl"],
            st["sk"], st["sv"], st["slf"], st["srw"], st["ssh"], st["sgl"])
```

```python
import functools

import jax
import jax.numpy as jnp
from jax import lax
from jax.experimental import pallas as pl
from jax.experimental.pallas import tpu as pltpu

F32 = jnp.float32
BF16 = jnp.bfloat16

RW_HD = 64
RW_DECAY_LORA = 32
RW_A_LORA = 32
RW_G_LORA = 96
RW_GN_EPS = 64e-5
FX_HD = 128
GL_HEADS = 4
GL_GATE_LORA = 16
GL_TAU = 16.0
NORM_EPS = 1e-6
PAGE = 128
LOG2E = 1.4426950408889634

LANES = 128
VMEM_LIMIT = 56 * 1024 * 1024

NN = (((1,), (0,)), ((), ()))
NT = (((1,), (1,)), ((), ()))
TN = (((0,), (0,)), ((), ()))


def _split2(x):
    hi = x.astype(BF16)
    lo = (x - hi.astype(F32)).astype(BF16)
    return hi, lo


def _split3(x):
    hi = x.astype(BF16)
    r = x - hi.astype(F32)
    mid = r.astype(BF16)
    lo = (r - mid.astype(F32)).astype(BF16)
    return hi, mid, lo


def _mm(a, b, dims=NN, prec=1):
    dot = functools.partial(lax.dot_general, dimension_numbers=dims, preferred_element_type=F32)
    if prec == 1:
        return dot(a.astype(BF16), b.astype(BF16))
    if prec == 3:
        ah, al = _split2(a)
        bh, bl = _split2(b)
        return dot(ah, bh) + (dot(ah, bl) + dot(al, bh))
    if prec == "a":
        bb = b.astype(BF16)
        h, m, l = _split3(a)
        return dot(h, bb) + (dot(m, bb) + dot(l, bb))
    if prec == "b":
        ab = a.astype(BF16)
        h, m, l = _split3(b)
        return dot(ab, h) + (dot(ab, m) + dot(ab, l))
    raise ValueError(prec)


def _sigmoid(x):
    return 1.0 / (1.0 + jnp.exp(-x))


def _softplus(x):
    return jnp.maximum(x, 0.0) + jnp.log1p(jnp.exp(-jnp.abs(x)))


def _iota2(shape, axis):
    return lax.broadcasted_iota(jnp.int32, shape, axis)


def _log2(n):
    assert n & (n - 1) == 0
    return n.bit_length() - 1


def _onehot(mask, dtype=BF16):
    return jnp.where(mask, 1.0, 0.0).astype(dtype)


def _cparams(*sem):
    return pltpu.CompilerParams(dimension_semantics=sem, vmem_limit_bytes=VMEM_LIMIT)


def _dense_kernel(*refs, rms, swiglu, residual, stage):
    it = iter(refs)
    a_ref = next(it)
    g_ref = next(it) if rms else None
    w_ref = next(it)
    w2_ref = next(it) if swiglu else None
    r_ref = next(it) if residual else None
    o_ref = next(it)
    a_sc = next(it) if stage else None

    if stage:
        @pl.when(pl.program_id(1) == 0)
        def _():
            a = a_ref[...].astype(F32)
            if rms:
                a = a * lax.rsqrt(jnp.mean(a * a, axis=-1, keepdims=True) + NORM_EPS) * g_ref[...]
            a_sc[...] = a.astype(BF16)

        a = a_sc[...]
    else:
        a = a_ref[...]
    y = jnp.dot(a, w_ref[...], preferred_element_type=F32)
    if swiglu:
        u = jnp.dot(a, w2_ref[...], preferred_element_type=F32)
        y = y * _sigmoid(y) * u
    if residual:
        y = y + r_ref[...]
    o_ref[...] = y.astype(o_ref.dtype)


def _dense(a, w, *, tm, tn, g=None, residual=None, swiglu=False, out_dtype=F32, name):
    m, k = a.shape
    n = w.shape[1] // 2 if swiglu else w.shape[1]
    assert m % tm == 0 and n % tn == 0
    rms = g is not None
    stage = rms or a.dtype != BF16
    nj = n // tn
    in_specs = [pl.BlockSpec((tm, k), lambda i, j: (i, 0))]
    args = [a]
    if rms:
        in_specs.append(pl.BlockSpec((1, k), lambda i, j: (0, 0)))
        args.append(g.reshape(1, k))
    in_specs.append(pl.BlockSpec((k, tn), lambda i, j: (0, j)))
    args.append(w)
    if swiglu:
        in_specs.append(pl.BlockSpec((k, tn), lambda i, j: (0, j + nj)))
        args.append(w)
    if residual is not None:
        in_specs.append(pl.BlockSpec((tm, tn), lambda i, j: (i, j)))
        args.append(residual)
    return pl.pallas_call(
        functools.partial(_dense_kernel, rms=rms, swiglu=swiglu, residual=residual is not None, stage=stage),
        grid=(m // tm, nj),
        in_specs=in_specs,
        out_specs=pl.BlockSpec((tm, tn), lambda i, j: (i, j)),
        out_shape=jax.ShapeDtypeStruct((m, n), out_dtype),
        scratch_shapes=[pltpu.VMEM((tm, k), BF16)] if stage else [],
        compiler_params=_cparams("arbitrary", "arbitrary"),
        name=name,
    )(*args)


def _segsum64(x, ones_blk):
    parts = [
        _mm(x[:, LANES * i : LANES * (i + 1)], ones_blk, prec="a") for i in range(x.shape[1] // LANES)
    ]
    return jnp.concatenate(parts, axis=1)


def _head_ones():
    r = _iota2((LANES, LANES), 0) >> _log2(RW_HD)
    c = _iota2((LANES, LANES), 1) >> _log2(RW_HD)
    return _onehot(r == c)


def _rw_prep_kernel(
    pr_ref, pk_ref, pv_ref, pz_ref, sr_ref, sk_ref, sv_ref, sz_ref, vec_ref, muz_ref, lora_ref,
    r_o, k_o, v_o, kk_o, bb_o, lw_o, bonus_o, g_o,
    prev_r, prev_k, prev_v, prev_z,
):
    tb = pr_ref.shape[0]

    @pl.when(pl.program_id(1) == 0)
    def _():
        prev_r[...] = sr_ref[0]
        prev_k[...] = sk_ref[0]
        prev_v[...] = sv_ref[0]
        prev_z[...] = sz_ref[0]

    def shift(p_ref, prev, mu):
        p = p_ref[...]
        row = _iota2(p.shape, 0)
        p_prev = jnp.where(row == 0, prev[...], pltpu.roll(p, 1, 0))
        prev[...] = p[tb - 1 : tb, :]
        return p + mu * (p_prev - p)

    vec = vec_ref[...]
    w0, a0, k_k, k_a, r_k = vec[0:1], vec[1:2], vec[2:3], vec[3:4], vec[4:5]
    r = shift(pr_ref, prev_r, vec[5:6])
    k = shift(pk_ref, prev_k, vec[6:7])
    v = shift(pv_ref, prev_v, vec[7:8])
    z = shift(pz_ref, prev_z, muz_ref[...])

    lane = _iota2(z.shape, 1)
    in_w = lane < RW_DECAY_LORA
    in_a = (lane >= RW_DECAY_LORA) & (lane < RW_DECAY_LORA + RW_A_LORA)
    lora = lora_ref[...]
    lw_lin = _mm(jnp.where(in_w, jnp.tanh(z), 0.0), lora, prec=3)
    a_lin = _mm(jnp.where(in_a, z, 0.0), lora, prec=3)
    g = _mm(jnp.where(in_w | in_a, 0.0, _sigmoid(z)), lora, prec=3)

    lw = -jnp.exp(-_softplus(-(w0 + lw_lin)) - 0.5)
    a = _sigmoid(a0 + a_lin)
    ones_blk = _head_ones()
    kk = k * k_k
    kk = kk / jnp.maximum(jnp.sqrt(_segsum64(kk * kk, ones_blk)), 1e-12)
    k = k * (1.0 + (a - 1.0) * k_a)
    r_o[...] = r
    k_o[...] = k
    v_o[...] = v
    kk_o[...] = kk
    bb_o[...] = a * kk
    lw_o[...] = lw
    bonus_o[...] = _segsum64(r * k * r_k, ones_blk) * v
    g_o[...] = g


def _rw_prep(proj, col0, col_z, shift0, vec, muz, lora, *, nseq, t, tb, name):
    w = 512
    nt = t // tb
    row = lambda b, i: b * nt + i
    sr, sk, sv, sz = shift0
    in_specs = [
        pl.BlockSpec((tb, w), lambda b, i: (row(b, i), col0)),
        pl.BlockSpec((tb, w), lambda b, i: (row(b, i), col0 + 1)),
        pl.BlockSpec((tb, w), lambda b, i: (row(b, i), col0 + 2)),
        pl.BlockSpec((tb, 256), lambda b, i: (row(b, i), col_z)),
        pl.BlockSpec((1, 1, w), lambda b, i: (b, 0, 0)),
        pl.BlockSpec((1, 1, w), lambda b, i: (b, 0, 0)),
        pl.BlockSpec((1, 1, w), lambda b, i: (b, 0, 0)),
        pl.BlockSpec((1, 1, 256), lambda b, i: (b, 0, 0)),
        pl.BlockSpec((8, w), lambda b, i: (0, 0)),
        pl.BlockSpec((1, 256), lambda b, i: (0, 0)),
        pl.BlockSpec((256, w), lambda b, i: (0, 0)),
    ]
    out = jax.ShapeDtypeStruct((nseq * t, w), F32)
    return pl.pallas_call(
        _rw_prep_kernel,
        grid=(nseq, nt),
        in_specs=in_specs,
        out_specs=[pl.BlockSpec((tb, w), lambda b, i: (row(b, i), 0))] * 8,
        out_shape=[out] * 8,
        scratch_shapes=[pltpu.VMEM((1, w), F32)] * 3 + [pltpu.VMEM((1, 256), F32)],
        compiler_params=_cparams("arbitrary", "arbitrary"),
        name=name,
    )(proj, proj, proj, proj, sr, sk, sv, sz, vec, muz, lora)


RW_PREC_SOLVE = 1
RW_PREC_APPLY = 1


def _rw_scan_kernel(
    r_ref, k_ref, v_ref, kk_ref, bb_ref, lw_ref, bonus_ref, g_ref, ln_ref, s0_ref,
    o_ref, st_ref, a_sc, *, c, valid, nchunks,
):
    ci = pl.program_id(1)

    @pl.when(ci == 0)
    def _():
        a_sc[...] = s0_ref[...]

    hp = LANES // RW_HD
    n = hp * c
    ti = _iota2((n, n), 0)
    si = _iota2((n, n), 1)
    same_head = (ti >> _log2(c)) == (si >> _log2(c))
    incl = same_head & (si <= ti)
    strict = same_head & (si < ti)
    tri = _onehot(_iota2((c, c), 1) <= _iota2((c, c), 0))
    ri = _iota2((LANES, LANES), 0)
    cj = _iota2((LANES, LANES), 1)
    eye = ri == cj
    blockdiag = (ri >> _log2(RW_HD)) == (cj >> _log2(RW_HD))
    ones_blk = _onehot(blockdiag)
    own_lane = (_iota2((n, 1), 0) >> _log2(c)) == (_iota2((1, LANES), 1) >> _log2(RW_HD))
    live = None
    if valid < c:
        live = (_iota2((c, 1), 0) + ci * c) < valid
    mms = functools.partial(_mm, prec=RW_PREC_SOLVE)
    mma = functools.partial(_mm, prec=RW_PREC_APPLY)

    def dup(x):
        return jnp.concatenate([x] * hp, axis=0)

    def stack(x):
        return jnp.where(own_lane, dup(x), 0.0)

    def fold(x):
        out = x[0:c]
        for h in range(1, hp):
            out = out + x[h * c : (h + 1) * c]
        return out

    nsb, npair = r_ref.shape[0], r_ref.shape[2] // LANES
    chains = [(s, p) for s in range(nsb) for p in range(npair)]
    tiles = range(len(chains))

    def load(ref, masked):
        out = []
        for s, p in chains:
            x = ref[s, :, LANES * p : LANES * (p + 1)]
            out.append(jnp.where(live, x, 0.0) if (masked and live is not None) else x)
        return out

    r_l, v_l = load(r_ref, False), load(v_ref, False)
    k_l, kk_l, bb_l, lw_l = load(k_ref, True), load(kk_ref, True), load(bb_ref, True), load(lw_ref, True)
    lwc_l = [_mm(tri, lw, prec="b") for lw in lw_l]
    lwe_l = [lwc[c - 1 : c, :] for lwc in lwc_l]
    a_st_l, r_t_l, v_st_l, k_h_l, b_h_l, gram_l = [], [], [], [], [], []
    for p in tiles:
        lwc, lwe = lwc_l[p], lwe_l[p]
        e_neg = jnp.exp(-lwc)
        e_end = jnp.exp(lwe - lwc)
        a_st = stack(kk_l[p] * jnp.exp(lwc - lw_l[p]))
        r_t = r_l[p] * jnp.exp(lwc)
        a_st_l.append(a_st)
        r_t_l.append(r_t)
        v_st_l.append(stack(v_l[p]))
        k_h_l.append(k_l[p] * e_end)
        b_h_l.append(bb_l[p] * e_end)
        gram_l.append(mms(jnp.concatenate([a_st, stack(r_t)], axis=0),
                          jnp.concatenate([dup(bb_l[p] * e_neg), dup(k_l[p] * e_neg)], axis=0), NT))
    l_l = [jnp.where(strict, g[:n, :n], 0.0) for g in gram_l]
    mv_l = [mma(jnp.where(strict, g[:n, n:], 0.0), v_st) for g, v_st in zip(gram_l, v_st_l)]
    t_l = [_onehot(ti == si, F32) for _ in tiles]
    lev = 0
    while (1 << lev) < c:
        off = ((ti >> (lev + 1)) == (si >> (lev + 1))) & ((ti >> lev) != (si >> lev))
        tl_l = [mms(t, jnp.where(off, l, 0.0)) for t, l in zip(t_l, l_l)]
        t_l = [t - mms(tl, t) for t, tl in zip(t_l, tl_l)]
        lev += 1
    z_l = [mma(t, jnp.concatenate([a_st, mv], axis=1)) for t, a_st, mv in zip(t_l, a_st_l, mv_l)]
    qz_l = [mma(jnp.where(incl, g[n:, :n], 0.0), z) for g, z in zip(gram_l, z_l)]
    pv_l = [mma(jnp.where(incl, g[n:, n:], 0.0), v_st) for g, v_st in zip(gram_l, v_st_l)]
    bz_l = [mma(b_h, jnp.concatenate([fold(z[:, :LANES]), fold(z[:, LANES:])], axis=1), TN)
            for b_h, z in zip(b_h_l, z_l)]
    kv_l = [mma(k_h, v, TN) for k_h, v in zip(k_h_l, v_l)]
    ya_l = []
    for p in tiles:
        ry = r_t_l[p] - fold(qz_l[p][:, :LANES])
        g_mat = jnp.where(eye, jnp.exp(lwe_l[p]), 0.0) - jnp.where(blockdiag, bz_l[p][:, :LANES], 0.0)
        ya_l.append(mma(jnp.concatenate([ry, g_mat], axis=0), a_sc[chains[p]]))
    ys = [ya[:c] + fold(pv - qz[:, LANES:]) for ya, pv, qz in zip(ya_l, pv_l, qz_l)]
    states = [ya[c:] + jnp.where(blockdiag, kv - bz[:, LANES:], 0.0) for ya, kv, bz in zip(ya_l, kv_l, bz_l)]
    a_sc[...] = jnp.stack(states).reshape(a_sc.shape)
    y = jnp.concatenate([jnp.concatenate(ys[s * npair : (s + 1) * npair], axis=1) for s in range(nsb)], axis=0)
    mean = _segsum64(y, ones_blk) * (1.0 / RW_HD)
    d = y - mean
    var = _segsum64(d * d, ones_blk) * (1.0 / RW_HD)
    yn = (d * lax.rsqrt(var + RW_GN_EPS)).reshape(o_ref.shape)
    o_ref[...] = (yn * ln_ref[0:1, :] + ln_ref[1:2, :] + bonus_ref[...]) * g_ref[...]

    @pl.when(ci == nchunks - 1)
    def _():
        st_ref[...] = a_sc[...]


def _rw_scan(prep, ln, s0, *, nseq, t, c, valid, name):
    w = 512
    nchunks = -(-valid // c)
    npair = w // LANES
    nsb = RW_SEQS_PER_STEP if nseq % RW_SEQS_PER_STEP == 0 else 1
    blk = pl.BlockSpec((nsb, c, w), lambda b, i: (b, i, 0))
    st_spec = pl.BlockSpec((nsb, npair, LANES, LANES), lambda b, i: (b, 0, 0, 0))
    o, st = pl.pallas_call(
        functools.partial(_rw_scan_kernel, c=c, valid=valid, nchunks=nchunks),
        grid=(nseq // nsb, nchunks),
        in_specs=[blk] * 8 + [pl.BlockSpec((2, w), lambda b, i: (0, 0)), st_spec],
        out_specs=[blk, st_spec],
        out_shape=[
            jax.ShapeDtypeStruct((nseq, nchunks * c, w), F32),
            jax.ShapeDtypeStruct((nseq, npair, LANES, LANES), F32),
        ],
        scratch_shapes=[pltpu.VMEM((nsb, npair, LANES, LANES), F32)],
        compiler_params=_cparams("arbitrary", "arbitrary"),
        name=name,
    )(*[a.reshape(nseq, t, w) for a in prep], ln, s0)
    return o.reshape(nseq * nchunks * c, w), st


GL_PREC = 3


def _gla_kernel(*refs, tb, valid, nblocks, dk, nsb):
    seq_refs = [refs[5 * s : 5 * s + 5] for s in range(nsb)]
    ab_ref, abias_ref, ng_ref, s0_ref, o_ref, st_ref, s_sc = refs[5 * nsb :]
    bi = pl.program_id(1)

    @pl.when(bi == 0)
    def _():
        s_sc[...] = s0_ref[...]

    mm = functools.partial(_mm, prec=GL_PREC)
    ti = _iota2((tb, tb), 0)
    si = _iota2((tb, tb), 1)
    tri = _onehot(si <= ti)
    trow = _iota2((tb, 1), 0)
    lane_head = _iota2((1, LANES), 1) >> _log2(dk)
    heads_per_tile = LANES // dk
    own_lane = (_iota2((heads_per_tile * tb, 1), 0) >> _log2(tb)) == lane_head

    def dup(x):
        return jnp.concatenate([x] * heads_per_tile, axis=0)

    def stack(x):
        return jnp.where(own_lane, dup(x), 0.0)

    npair = seq_refs[0][0].shape[1] // LANES
    chains = [(s, p) for s in range(nsb) for p in range(npair)]
    q_l, k_l, la_l = [], [], []
    for s in range(nsb):
        q_ref, k_ref, _, xa_ref, _ = seq_refs[s]
        la_all = -_softplus(-(_mm(xa_ref[...], ab_ref[...], prec=3) + abias_ref[...])) * (1.0 / GL_TAU)
        k_all = k_ref[...]
        if valid < tb:
            live = (trow + bi * tb) < valid
            la_all = jnp.where(live, la_all, 0.0)
            k_all = jnp.where(live, k_all, 0.0)
        for p in range(npair):
            sl = slice(LANES * p, LANES * (p + 1))
            q_l.append(q_ref[:, sl] * (dk**-0.5))
            k_l.append(k_all[:, sl])
            la_l.append(la_all[:, sl])
    b_l = [_mm(tri, la, prec="b") for la in la_l]
    att_l = [jnp.where(dup(ti == si), mm(stack(q), k, NT), 0.0) for q, k in zip(q_l, k_l)]
    lev = 1
    while (1 << lev) <= tb:
        half = 1 << (lev - 1)
        sel = _onehot(si == ((ti >> lev) << lev) + (half - 1))
        upper = ((trow >> (lev - 1)) & 1) == 1
        same = dup((ti >> lev) == (si >> lev))
        bref_l = [_mm(sel, b, prec="b") for b in b_l]
        qs_l = [stack(jnp.where(upper, q * jnp.exp(jnp.minimum(b - br, 0.0)), 0.0)) for q, b, br in zip(q_l, b_l, bref_l)]
        ks_l = [jnp.where(upper, 0.0, k * jnp.exp(jnp.minimum(br - b, 0.0))) for k, b, br in zip(k_l, b_l, bref_l)]
        att_l = [att + jnp.where(same, mm(qs, ks, NT), 0.0) for att, qs, ks in zip(att_l, qs_l, ks_l)]
        lev += 1

    s_old_l = [s_sc[s, p] for s, p in chains]
    inter_l = [mm(stack(q * jnp.exp(b)), s_old, NT) for q, b, s_old in zip(q_l, b_l, s_old_l)]
    outs, states = [], []
    for i, (s, p) in enumerate(chains):
        b_end = b_l[i][tb - 1 : tb, :]
        ke = k_l[i] * jnp.exp(b_end - b_l[i])
        s_new = s_old_l[i] * jnp.exp(b_end)
        for h in range(heads_per_tile):
            hv = heads_per_tile * p + h
            v = seq_refs[s][2][:, LANES * hv : LANES * (hv + 1)]
            o = mm(att_l[i][h * tb : (h + 1) * tb], v) + inter_l[i][h * tb : (h + 1) * tb]
            s_new = s_new + mm(v, jnp.where(lane_head == h, ke, 0.0), TN)
            outs.append(o * lax.rsqrt(jnp.mean(o * o, axis=-1, keepdims=True) + NORM_EPS) * ng_ref[...])
        states.append(s_new)

    s_sc[...] = jnp.stack(states).reshape(s_sc.shape)
    per_seq = len(outs) // nsb
    for s in range(nsb):
        gg = seq_refs[s][4][...]
        o_ref[s] = jnp.concatenate(outs[s * per_seq : (s + 1) * per_seq], axis=1) * (gg * _sigmoid(gg))

    @pl.when(bi == nblocks - 1)
    def _():
        st_ref[...] = s_sc[...]


def _gla(proj, cols, ab, abias, ng, s0, *, nseq, t, tb, valid, name):
    cq, ck, cv, cxa, cgg = cols
    nb = t // tb
    nrun = -(-valid // tb)
    npair = 256 // LANES
    nsb = GL_SEQS_PER_STEP if nseq % GL_SEQS_PER_STEP == 0 else 1
    st_spec = pl.BlockSpec((nsb, npair, LANES, LANES), lambda b, i: (b, 0, 0, 0))

    def seq_specs(s):
        row = lambda b, i: (b * nsb + s) * nb + i
        return [
            pl.BlockSpec((tb, 256), lambda b, i: (row(b, i), cq)),
            pl.BlockSpec((tb, 256), lambda b, i: (row(b, i), ck)),
            pl.BlockSpec((tb, 512), lambda b, i: (row(b, i), cv)),
            pl.BlockSpec((tb, LANES), lambda b, i: (row(b, i), cxa)),
            pl.BlockSpec((tb, 512), lambda b, i: (row(b, i), cgg)),
        ]

    o, st = pl.pallas_call(
        functools.partial(_gla_kernel, tb=tb, valid=valid, nblocks=nrun, dk=256 // GL_HEADS, nsb=nsb),
        grid=(nseq // nsb, nrun),
        in_specs=[spec for s in range(nsb) for spec in seq_specs(s)]
        + [
            pl.BlockSpec((LANES, 256), lambda b, i: (0, 0)),
            pl.BlockSpec((1, 256), lambda b, i: (0, 0)),
            pl.BlockSpec((1, LANES), lambda b, i: (0, 0)),
            st_spec,
        ],
        out_specs=[pl.BlockSpec((nsb, tb, 512), lambda b, i: (b, i, 0)), st_spec],
        out_shape=[
            jax.ShapeDtypeStruct((nseq, nrun * tb, 512), F32),
            jax.ShapeDtypeStruct((nseq, npair, LANES, LANES), F32),
        ],
        scratch_shapes=[pltpu.VMEM((nsb, npair, LANES, LANES), F32)],
        compiler_params=_cparams("arbitrary", "arbitrary"),
        name=name,
    )(*([proj] * (5 * nsb)), ab, abias, ng, s0)
    return o.reshape(nseq * nrun * tb, 512), st


def _fx_prep_kernel(*refs, stacked):
    q_ref, k_ref, v_ref, fl_ref, qg_ref, kg_ref, bf_ref = refs[:7]
    qb_o, kf_o, vf_o, kb_o, vb_o, lf_o, c_o, carry = refs[9:] if stacked else refs[7:]
    tb = q_ref.shape[0]

    @pl.when(pl.program_id(1) == 0)
    def _():
        carry[...] = jnp.zeros_like(carry)

    def headnorm(x, gain):
        return x * lax.rsqrt(jnp.mean(x * x, axis=-1, keepdims=True) + NORM_EPS) * gain

    for h in range(q_ref.shape[1] // FX_HD):
        sl = slice(FX_HD * h, FX_HD * (h + 1))
        qb_o[:, sl] = (headnorm(q_ref[:, sl], qg_ref[...]) * (FX_HD**-0.5 * LOG2E)).astype(BF16)
        kn = headnorm(k_ref[:, sl], kg_ref[...])
        kf_o[:, sl] = kn
        kb_o[:, sl] = kn.astype(BF16)
    v = v_ref[...]
    vf_o[...] = v
    vb_o[...] = v.astype(BF16)
    logf = -_softplus(-(fl_ref[...] + bf_ref[...]))
    lf_o[...] = logf
    tri = _onehot(_iota2((tb, tb), 1) <= _iota2((tb, tb), 0))
    c = _mm(tri, logf, prec="b") + carry[...]
    c_o[...] = c
    carry[...] = c[tb - 1 : tb, :]


def _fx_prep(proj, cols, qg, kg, bf, *, nseq, t, tb, name, stack=None):
    cq, ck, cv, cfl = cols
    nt = t // tb
    row = lambda b, i: b * nt + i
    w = 1024
    wide = pl.BlockSpec((tb, w), lambda b, i: (row(b, i), 0))
    thin = pl.BlockSpec((tb, LANES), lambda b, i: (row(b, i), 0))
    vecspec = pl.BlockSpec((1, LANES), lambda b, i: (0, 0))
    rows = nseq * t
    flat_f32 = jax.ShapeDtypeStruct((rows, w), F32)
    extra_in, extra_args, aliases = [], [], {}
    kv_spec, kv_shape = wide, flat_f32
    if stack is not None:
        k_all, v_all, layer = stack
        extra_in = [pl.BlockSpec(memory_space=pl.ANY)] * 2
        extra_args = [k_all, v_all]
        aliases = {7: 1, 8: 2}
        kv_spec = pl.BlockSpec((None, None, tb, w), lambda b, i: (b, layer, i, 0))
        kv_shape = jax.ShapeDtypeStruct(k_all.shape, F32)
    return pl.pallas_call(
        functools.partial(_fx_prep_kernel, stacked=stack is not None),
        grid=(nseq, nt),
        in_specs=[
            pl.BlockSpec((tb, w), lambda b, i: (row(b, i), cq)),
            pl.BlockSpec((tb, w), lambda b, i: (row(b, i), ck)),
            pl.BlockSpec((tb, w), lambda b, i: (row(b, i), cv)),
            pl.BlockSpec((tb, LANES), lambda b, i: (row(b, i), cfl)),
            vecspec, vecspec, vecspec,
        ] + extra_in,
        out_specs=[wide, kv_spec, kv_spec, wide, wide, thin, thin],
        out_shape=[
            jax.ShapeDtypeStruct((rows, w), BF16),
            kv_shape,
            kv_shape,
            jax.ShapeDtypeStruct((rows, w), BF16),
            jax.ShapeDtypeStruct((rows, w), BF16),
            jax.ShapeDtypeStruct((rows, LANES), F32),
            jax.ShapeDtypeStruct((rows, LANES), F32),
        ],
        scratch_shapes=[pltpu.VMEM((1, LANES), F32)],
        input_output_aliases=aliases,
        compiler_params=_cparams("arbitrary", "arbitrary"),
        name=name,
    )(proj, proj, proj, proj, qg, kg, bf, *extra_args)


def _with_ones(v):
    return jnp.concatenate([v, jnp.ones_like(v)], axis=1)


def _flash_kernel(q_ref, k_ref, v_ref, c_ref, cref_ref, o_ref, m_sc, acc_sc, *, tq, tk, hb):
    qi = pl.program_id(2)
    kj = pl.program_id(3)

    @pl.when(kj == 0)
    def _():
        m_sc[...] = jnp.full_like(m_sc, -jnp.inf)
        acc_sc[...] = jnp.zeros_like(acc_sc)

    def update(masked):
        heads = range(hb)
        sl = [slice(FX_HD * h, FX_HD * (h + 1)) for h in heads]
        s_l = [lax.dot_general(q_ref[:, sl[h]], k_ref[:, sl[h]], NT, preferred_element_type=F32) for h in heads]
        s_l = [s_l[h] + (cref_ref[0, h, :, 0:1] - c_ref[0, h]) * LOG2E for h in heads]
        if masked:
            keep = (kj * tk + _iota2((tq, tk), 1)) <= (qi * tq + _iota2((tq, tk), 0))
            s_l = [jnp.where(keep, s, -jnp.inf) for s in s_l]
        m_old = [m_sc[h] for h in heads]
        m_new = [jnp.maximum(m_old[h], jnp.max(s_l[h], axis=-1, keepdims=True)) for h in heads]
        p_l = [jnp.exp2(s_l[h] - pltpu.repeat(m_new[h], tk // LANES, axis=1)) for h in heads]
        pv_l = [jnp.dot(p_l[h].astype(BF16), _with_ones(v_ref[:, sl[h]]), preferred_element_type=F32) for h in heads]
        for h in heads:
            alpha = jnp.exp2(m_old[h] - m_new[h])
            acc_sc[h] = pltpu.repeat(alpha, 2, axis=1) * acc_sc[h] + pv_l[h]
            m_sc[h] = m_new[h]

    first_key, last_key = kj * tk, kj * tk + (tk - 1)
    first_query, last_query = qi * tq, qi * tq + (tq - 1)

    @pl.when(last_key <= first_query)
    def _():
        update(False)

    @pl.when((last_key > first_query) & (first_key <= last_query))
    def _():
        update(True)

    @pl.when(kj == pl.num_programs(3) - 1)
    def _():
        for h in range(hb):
            acc = acc_sc[h]
            o_ref[:, FX_HD * h : FX_HD * (h + 1)] = acc[:, :FX_HD] / acc[:, FX_HD:]


def _flash(qb, kb, vb, c_row, *, nseq, t, tq, tk, name):
    nh = qb.shape[1] // FX_HD
    hb = FLASH_HEADS_PER_STEP if nh % FLASH_HEADS_PER_STEP == 0 else 1
    nq, nk = t // tq, t // tk
    last = lambda i: (i * tq + tq - 1) // tk
    w = hb * FX_HD
    return pl.pallas_call(
        functools.partial(_flash_kernel, tq=tq, tk=tk, hb=hb),
        grid=(nseq, nh // hb, nq, nk),
        in_specs=[
            pl.BlockSpec((tq, w), lambda b, h, i, j: (b * nq + i, h)),
            pl.BlockSpec((tk, w), lambda b, h, i, j: (b * nk + jnp.minimum(j, last(i)), h)),
            pl.BlockSpec((tk, w), lambda b, h, i, j: (b * nk + jnp.minimum(j, last(i)), h)),
            pl.BlockSpec((1, hb, 1, tk), lambda b, h, i, j: (b, h, 0, jnp.minimum(j, last(i)))),
            pl.BlockSpec((1, hb, 1, LANES), lambda b, h, i, j: (b, h, 0, i * (tq // LANES))),
        ],
        out_specs=pl.BlockSpec((tq, w), lambda b, h, i, j: (b * nq + i, h)),
        out_shape=jax.ShapeDtypeStruct((nseq * t, nh * FX_HD), F32),
        scratch_shapes=[pltpu.VMEM((hb, tq, LANES), F32), pltpu.VMEM((hb, tq, 2 * FX_HD), F32)],
        compiler_params=_cparams("arbitrary", "arbitrary", "arbitrary", "arbitrary"),
        name=name,
    )(qb, kb, vb, c_row, c_row)


def _decode_kernel(pt_ref, q_ref, kn_ref, vn_ref, cn_ref, *rest, nq, nh, pp):
    kc_refs, vc_refs, lf_refs = rest[0:pp], rest[pp : 2 * pp], rest[2 * pp : 3 * pp]
    o_ref, m_sc, acc_sc, carry = rest[3 * pp :]
    j = pl.program_id(1)
    heads = range(nh)
    q_l = [q_ref[:, FX_HD * h : FX_HD * (h + 1)] for h in heads]

    def attend(pages, causal):
        kv = [[load(h) for h in heads] for load, _ in pages]
        s = [[lax.dot_general(q_l[h], kv[i][h][0], NT, preferred_element_type=F32) + pages[i][1][h : h + 1, :]
              for h in heads] for i in range(len(pages))]
        if causal:
            keep = _iota2((nq, PAGE), 1) <= _iota2((nq, PAGE), 0)
            s = [[jnp.where(keep, x, -jnp.inf) for x in row] for row in s]
        for h in heads:
            m_old = m_sc[h]
            m_new = m_old
            for i in range(len(pages)):
                m_new = jnp.maximum(m_new, jnp.max(s[i][h], axis=-1, keepdims=True))
            acc = pltpu.repeat(jnp.exp2(m_old - m_new), 2, axis=1) * acc_sc[h]
            for i in range(len(pages)):
                p = jnp.exp2(s[i][h] - m_new)
                acc = acc + jnp.dot(p.astype(BF16), _with_ones(kv[i][h][1]), preferred_element_type=F32)
            acc_sc[h] = acc
            m_sc[h] = m_new

    @pl.when(j == 0)
    def _():
        m_sc[...] = jnp.full_like(m_sc, -jnp.inf)
        acc_sc[...] = jnp.zeros_like(acc_sc)
        carry[...] = jnp.zeros_like(carry)

        def new_rows(h):
            sl = slice(FX_HD * h, FX_HD * (h + 1))
            return kn_ref[:, sl], vn_ref[:, sl]

        attend([(new_rows, -cn_ref[0] * LOG2E)], True)

    @pl.when(j > 0)
    def _():
        after = _onehot(_iota2((PAGE, PAGE), 0) > _iota2((PAGE, PAGE), 1))
        both = jnp.concatenate([after, jnp.ones((PAGE, PAGE), BF16)], axis=1)
        run = carry[...]
        pages = []
        for i in range(pp):

            def cached_rows(h, i=i):
                rows = pl.ds(h, PAGE, stride=nh)
                return kc_refs[i][rows, :].astype(BF16), vc_refs[i][rows, :].astype(BF16)

            sums = _mm(lf_refs[i][...], both, TN, prec="a")
            pages.append((cached_rows, (run + sums[:, :PAGE]) * LOG2E))
            run = run + sums[:, PAGE:]
        carry[...] = run
        attend(pages, False)

    @pl.when(j == pl.num_programs(1) - 1)
    def _():
        for h in heads:
            acc = acc_sc[h]
            o_ref[:, FX_HD * h : FX_HD * (h + 1)] = acc[:, :FX_HD] / acc[:, FX_HD:]


def _decode(qb, kb, vb, cn_t, cache_k, cache_v, cache_logf, page_table, layer, *, nq, name):
    nseq, npages = page_table.shape
    pool, depth, _, nh, hd = cache_k.shape
    w = nh * hd
    pp = DEC_PAGES_PER_STEP if npages % DEC_PAGES_PER_STEP == 0 else 1
    ck = cache_k.reshape(pool, depth, PAGE * nh, hd)
    cv = cache_v.reshape(pool, depth, PAGE * nh, hd)

    def page_of(i):
        def index(b, j, pt):
            back = (jnp.maximum(j, 1) - 1) * pp + i
            return (pt[b * npages + (npages - 1 - back)], layer, 0, 0)

        return index

    grid_spec = pltpu.PrefetchScalarGridSpec(
        num_scalar_prefetch=1,
        grid=(nseq, 1 + npages // pp),
        in_specs=[
            pl.BlockSpec((nq, w), lambda b, j, pt: (b * (PAGE // nq), 0)),
            pl.BlockSpec((PAGE, w), lambda b, j, pt: (b, 0)),
            pl.BlockSpec((PAGE, w), lambda b, j, pt: (b, 0)),
            pl.BlockSpec((1, nh, PAGE), lambda b, j, pt: (b, 0, 0)),
        ]
        + [pl.BlockSpec((None, None, PAGE * nh, hd), page_of(i)) for i in range(pp)] * 2
        + [pl.BlockSpec((None, None, PAGE, nh), page_of(i)) for i in range(pp)],
        out_specs=pl.BlockSpec((nq, w), lambda b, j, pt: (b, 0)),
        scratch_shapes=[
            pltpu.VMEM((nh, nq, LANES), F32),
            pltpu.VMEM((nh, nq, 2 * hd), F32),
            pltpu.VMEM((nh, PAGE), F32),
        ],
    )
    return pl.pallas_call(
        functools.partial(_decode_kernel, nq=nq, nh=nh, pp=pp),
        grid_spec=grid_spec,
        out_shape=jax.ShapeDtypeStruct((nseq * nq, w), F32),
        compiler_params=_cparams("arbitrary", "arbitrary"),
        name=name,
    )(page_table.reshape(-1), qb, kb, vb, cn_t, *([ck] * pp), *([cv] * pp), *([cache_logf] * pp))


RW_CHUNK = 64
RW_SEQS_PER_STEP = 2
GL_BLOCK = 128
GL_SEQS_PER_STEP = 2
FLASH_HEADS_PER_STEP = 2
DEC_Q_ROWS = 16
DEC_PAGES_PER_STEP = 4
TM = 768
COL_FX_Q, COL_FX_K, COL_FX_V = 0, 1024, 2048
COL_RW_R = 3072
COL_GL_V, COL_GL_GG = 4608, 5120
COL_GL_Q, COL_GL_K = 5632, 5888
COL_RW_Z = 6144
COL_FX_FL = 6400
COL_GL_XA = 6528
COLS_PAD = 6656


def _pad_cols(x, width):
    return jnp.pad(x, [(0, 0)] * (x.ndim - 1) + [(0, width - x.shape[-1])])


def _relayout_in(w, rw_w, fx_w, gl_kw, gl_vw, nh_fx):
    rw_z = RW_DECAY_LORA + RW_A_LORA + RW_G_LORA
    o = 0
    rw_rkv = w[..., o : o + 3 * rw_w]; o += 3 * rw_w
    rw_zz = w[..., o : o + rw_z]; o += rw_z
    fx_qkv = w[..., o : o + 3 * fx_w]; o += 3 * fx_w
    fx_fl = w[..., o : o + nh_fx]; o += nh_fx
    gl_qk = w[..., o : o + 2 * gl_kw]; o += 2 * gl_kw
    gl_v = w[..., o : o + gl_vw]; o += gl_vw
    gl_xa = w[..., o : o + GL_GATE_LORA]; o += GL_GATE_LORA
    gl_gg = w[..., o : o + gl_vw]; o += gl_vw
    assert o == w.shape[-1]
    return jnp.concatenate(
        [fx_qkv, rw_rkv, gl_v, gl_gg, gl_qk, _pad_cols(rw_zz, 256), _pad_cols(fx_fl, LANES), _pad_cols(gl_xa, LANES)],
        axis=-1,
    )


def _mixers(proj, layer, lp, states, paged, *, nseq, t, valid, tag, kv_stack=None):
    (shift0, rw_s0, gl_s0) = states
    tb = min(t, 256)
    prep = _rw_prep(proj, COL_RW_R // 512, COL_RW_Z // 256, shift0, lp["rw_vec"], lp["rw_muz"], lp["rw_lora"],
                    nseq=nseq, t=t, tb=tb, name=f"rw_prep_{tag}")
    o_rw, rw_state = _rw_scan(prep, lp["rw_ln"], rw_s0, nseq=nseq, t=t, c=min(t, RW_CHUNK), valid=valid,
                              name=f"rw_scan_{tag}")
    o_gl, gl_state = _gla(
        proj, (COL_GL_Q // 256, COL_GL_K // 256, COL_GL_V // 512, COL_GL_XA // LANES, COL_GL_GG // 512),
        lp["gl_ab"], lp["gl_abias"], lp["gl_ng"], gl_s0, nseq=nseq, t=t, tb=min(t, GL_BLOCK), valid=valid,
        name=f"gla_{tag}")
    qb, kf, vf, kb, vb, logf, c = _fx_prep(
        proj, (COL_FX_Q // 1024, COL_FX_K // 1024, COL_FX_V // 1024, COL_FX_FL // LANES),
        lp["fx_qg"], lp["fx_kg"], lp["fx_bf"], nseq=nseq, t=t, tb=tb, name=f"fx_prep_{tag}",
        stack=None if kv_stack is None else (*kv_stack, layer))
    nh = kb.shape[1] // FX_HD
    c_t = jnp.transpose(c.reshape(nseq, t, LANES)[:, :, :nh], (0, 2, 1))
    if paged is None:
        blk = min(t, 512)
        o_fx = _flash(qb, kb, vb, c_t[:, :, None, :], nseq=nseq, t=t, tq=blk, tk=blk, name=f"flash_{tag}")
    else:
        cache_k, cache_v, cache_logf, page_table = paged
        nq = DEC_Q_ROWS
        assert t == PAGE and valid <= nq
        o_fx = _decode(qb, kb, vb, c_t, cache_k, cache_v, cache_logf, page_table, layer, nq=nq, name=f"decode_{tag}")
        o_fx = o_fx.reshape(nseq, nq, -1)[:, :valid].reshape(nseq * valid, -1)
    return o_rw, o_fx, o_gl, kf, logf, rw_state, gl_state, vf


def kernel(x_prompt, x_sample, cache_k, cache_v, cache_logf, state_rwkv, state_rwkv_shift, state_gla,
           page_table, attn_norm_g, w_in, rw_mu, rw_w_b, rw_w0, rw_a_b, rw_a0, rw_g_b, rw_k_k, rw_k_a,
           rw_r_k, rw_lnx_g, rw_lnx_b, fx_qn_g, fx_kn_g, fx_b_f, gl_alpha_b, gl_alpha_bias, gl_norm_g,
           w_out, ffn_norm_g, w_gu, w_down):
    nb, seq, d_model = x_prompt.shape
    ns, dseq, _ = x_sample.shape
    depth = w_in.shape[0]
    rw_w = rw_w0.shape[1]
    rw_heads = rw_w // RW_HD
    nh_fx = fx_b_f.shape[1]
    fx_w = nh_fx * FX_HD
    gl_kw = gl_alpha_b.shape[2]
    gl_dk = gl_kw // GL_HEADS
    gl_dv = gl_norm_g.shape[1]
    gl_vw = GL_HEADS * gl_dv
    rw_cols = 3 * rw_w + RW_DECAY_LORA + RW_A_LORA + RW_G_LORA
    assert (rw_w, fx_w, gl_kw, gl_vw, FX_HD, gl_dv) == (512, 1024, 256, 512, 128, 128)
    n_prompt = nb * seq
    n_sample = ns * dseq
    rows = n_prompt + n_sample
    tm = TM if rows >= TM else 64
    rows_pad = -(-rows // tm) * tm
    relayout = functools.partial(_relayout_in, rw_w=rw_w, fx_w=fx_w, gl_kw=gl_kw, gl_vw=gl_vw, nh_fx=nh_fx)

    w_in_p = relayout(w_in).astype(BF16)
    w_out_b = w_out.astype(BF16)
    w_gu_b = w_gu.astype(BF16)
    w_down_b = w_down.astype(BF16)
    d_ff = w_down.shape[1]
    rw_mu_r, rw_mu_k, rw_mu_v, rw_mu_z = (
        rw_mu[:, :rw_w], rw_mu[:, rw_w : 2 * rw_w], rw_mu[:, 2 * rw_w : 3 * rw_w], rw_mu[:, 3 * rw_w :])
    layers = []
    for l in range(depth):
        layers.append(dict(
            rw_vec=jnp.stack([rw_w0[l], rw_a0[l], rw_k_k[l], rw_k_a[l], rw_r_k[l].reshape(-1),
                              rw_mu_r[l], rw_mu_k[l], rw_mu_v[l]]),
            rw_muz=_pad_cols(rw_mu_z[l][None], 256),
            rw_lora=jnp.pad(jnp.concatenate([rw_w_b[l], rw_a_b[l], rw_g_b[l]], axis=0),
                            ((0, 256 - (rw_cols - 3 * rw_w)), (0, 0))),
            rw_ln=jnp.stack([rw_lnx_g[l], rw_lnx_b[l]]),
            fx_qg=fx_qn_g[l][None], fx_kg=fx_kn_g[l][None], fx_bf=_pad_cols(fx_b_f[l][None], LANES),
            gl_ab=jnp.pad(gl_alpha_b[l], ((0, LANES - GL_GATE_LORA), (0, 0))),
            gl_abias=gl_alpha_bias[l][None], gl_ng=gl_norm_g[l][None],
        ))

    def rw_state_in(s):
        n = s.shape[0]
        a = jnp.swapaxes(s, -1, -2).reshape(n, rw_heads // 2, 2, RW_HD, RW_HD)
        z = jnp.zeros_like(a[:, :, 0])
        top = jnp.concatenate([a[:, :, 0], z], axis=-1)
        bot = jnp.concatenate([z, a[:, :, 1]], axis=-1)
        return jnp.concatenate([top, bot], axis=-2)

    def rw_state_out(a):
        n = a.shape[0]
        h0 = a[:, :, :RW_HD, :RW_HD]
        h1 = a[:, :, RW_HD:, RW_HD:]
        return jnp.swapaxes(jnp.stack([h0, h1], axis=2).reshape(n, rw_heads, RW_HD, RW_HD), -1, -2)

    def gl_state_in(s):
        n = s.shape[0]
        a = jnp.swapaxes(s, -1, -2).reshape(n, GL_HEADS // 2, 2, gl_dv, gl_dk)
        return jnp.transpose(a, (0, 1, 3, 2, 4)).reshape(n, GL_HEADS // 2, gl_dv, 2 * gl_dk)

    def gl_state_out(a):
        n = a.shape[0]
        a = jnp.transpose(a.reshape(n, GL_HEADS // 2, gl_dv, 2, gl_dk), (0, 1, 3, 2, 4))
        return jnp.swapaxes(a.reshape(n, GL_HEADS, gl_dv, gl_dk), -1, -2)

    def shift_in(s):
        return (s[:, None, :rw_w], s[:, None, rw_w : 2 * rw_w], s[:, None, 2 * rw_w : 3 * rw_w],
                _pad_cols(s[:, None, 3 * rw_w :], 256))

    def shift_out(proj_rows):
        return jnp.concatenate([proj_rows[:, COL_RW_R : COL_RW_R + 3 * rw_w],
                                proj_rows[:, COL_RW_Z : COL_RW_Z + rw_cols - 3 * rw_w]], axis=-1)

    x = jnp.concatenate([x_prompt.reshape(n_prompt, d_model), x_sample.reshape(n_sample, d_model),
                         jnp.zeros((rows_pad - rows, d_model), x_prompt.dtype)], axis=0)
    t_s = PAGE
    p_states = (shift_in(jnp.zeros((nb, rw_cols), F32)), jnp.zeros((nb, rw_heads // 2, LANES, LANES), F32),
                jnp.zeros((nb, GL_HEADS // 2, LANES, LANES), F32))
    outs = {k: [] for k in ("plf", "prw", "psh", "pgl", "sk", "sv", "slf", "srw", "ssh", "sgl")}
    pk_all = jnp.zeros((nb, depth, seq, fx_w), F32)
    pv_all = jnp.zeros((nb, depth, seq, fx_w), F32)
    for l in range(depth):
        lp = layers[l]
        proj = _dense(x, w_in_p[l], tm=tm, tn=512, g=attn_norm_g[l], name=f"w_in_{l}")
        proj_s = jnp.pad(proj[n_prompt:rows].reshape(ns, dseq, COLS_PAD), ((0, 0), (0, t_s - dseq), (0, 0)))
        proj_s = proj_s.reshape(ns * t_s, COLS_PAD)
        s_states = (shift_in(state_rwkv_shift[:, l]), rw_state_in(state_rwkv[:, l]), gl_state_in(state_gla[:, l]))
        po = _mixers(proj, l, lp, p_states, None, nseq=nb, t=seq, valid=seq, tag=f"p{l}", kv_stack=(pk_all, pv_all))
        pk_all, pv_all = po[3], po[7]
        so = _mixers(proj_s, l, lp, s_states, (cache_k, cache_v, cache_logf, page_table),
                     nseq=ns, t=t_s, valid=dseq, tag=f"s{l}")

        def unpad(a):
            return a.reshape(ns, a.shape[0] // ns, -1)[:, :dseq].reshape(n_sample, -1)

        mix = jnp.concatenate([
            jnp.concatenate([po[0], po[1], po[2]], axis=1),
            jnp.concatenate([unpad(so[0]), so[1], unpad(so[2])], axis=1),
            jnp.zeros((rows_pad - rows, d_model), F32)], axis=0)
        x = _dense(mix, w_out_b[l], tm=tm, tn=512, residual=x, name=f"w_out_{l}")
        act = _dense(x, w_gu_b[l], tm=tm, tn=512, g=ffn_norm_g[l], swiglu=True, out_dtype=BF16, name=f"w_gu_{l}")
        x = _dense(act, w_down_b[l], tm=tm, tn=256, residual=x, name=f"w_down_{l}")

        outs["plf"].append(po[4][:, :nh_fx].reshape(nb, seq, nh_fx))
        outs["prw"].append(rw_state_out(po[5]))
        outs["psh"].append(shift_out(proj[:n_prompt].reshape(nb, seq, COLS_PAD)[:, seq - 1]))
        outs["pgl"].append(gl_state_out(po[6]))
        outs["sk"].append(unpad(so[3]).reshape(ns, dseq, nh_fx, FX_HD))
        outs["sv"].append(unpad(so[7]).reshape(ns, dseq, nh_fx, FX_HD))
        outs["slf"].append(unpad(so[4])[:, :nh_fx].reshape(ns, dseq, nh_fx))
        outs["srw"].append(rw_state_out(so[5]))
        outs["ssh"].append(shift_out(proj[n_prompt:rows].reshape(ns, dseq, COLS_PAD)[:, dseq - 1]))
        outs["sgl"].append(gl_state_out(so[6]))

    st = {k: jnp.stack(v, axis=1) for k, v in outs.items()}
    y_prompt = x[:n_prompt].reshape(nb, seq, d_model)
    y_sample = x[n_prompt:rows].reshape(ns, dseq, d_model)
    kv_shape = (nb, depth, seq, nh_fx, FX_HD)
    return (y_prompt, y_sample, pk_all.reshape(kv_shape), pv_all.reshape(kv_shape), st["plf"], st["prw"], st["psh"], st["pgl"],
            st["sk"], st["sv"], st["slf"], st["srw"], st["ssh"], st["sgl"])
```

```python
import functools

import jax
import jax.numpy as jnp
from jax import lax
from jax.experimental import pallas as pl
from jax.experimental.pallas import tpu as pltpu

F32 = jnp.float32
BF16 = jnp.bfloat16

RW_HD = 64
RW_DECAY_LORA = 32
RW_A_LORA = 32
RW_G_LORA = 96
RW_GN_EPS = 64e-5
FX_HD = 128
GL_HEADS = 4
GL_GATE_LORA = 16
GL_TAU = 16.0
NORM_EPS = 1e-6
PAGE = 128
LOG2E = 1.4426950408889634

LANES = 128
VMEM_LIMIT = 56 * 1024 * 1024

NN = (((1,), (0,)), ((), ()))
NT = (((1,), (1,)), ((), ()))
TN = (((0,), (0,)), ((), ()))


def _split2(x):
    hi = x.astype(BF16)
    lo = (x - hi.astype(F32)).astype(BF16)
    return hi, lo


def _split3(x):
    hi = x.astype(BF16)
    r = x - hi.astype(F32)
    mid = r.astype(BF16)
    lo = (r - mid.astype(F32)).astype(BF16)
    return hi, mid, lo


def _mm(a, b, dims=NN, prec=1):
    dot = functools.partial(lax.dot_general, dimension_numbers=dims, preferred_element_type=F32)
    if prec == 1:
        return dot(a.astype(BF16), b.astype(BF16))
    if prec == 3:
        ah, al = _split2(a)
        bh, bl = _split2(b)
        return dot(ah, bh) + (dot(ah, bl) + dot(al, bh))
    if prec == "a":
        bb = b.astype(BF16)
        h, m, l = _split3(a)
        return dot(h, bb) + (dot(m, bb) + dot(l, bb))
    if prec == "b":
        ab = a.astype(BF16)
        h, m, l = _split3(b)
        return dot(ab, h) + (dot(ab, m) + dot(ab, l))
    raise ValueError(prec)


def _sigmoid(x):
    return 1.0 / (1.0 + jnp.exp(-x))


def _softplus(x):
    return jnp.maximum(x, 0.0) + jnp.log1p(jnp.exp(-jnp.abs(x)))


def _iota2(shape, axis):
    return lax.broadcasted_iota(jnp.int32, shape, axis)


def _log2(n):
    assert n & (n - 1) == 0
    return n.bit_length() - 1


def _onehot(mask, dtype=BF16):
    return jnp.where(mask, 1.0, 0.0).astype(dtype)


def _cparams(*sem):
    return pltpu.CompilerParams(dimension_semantics=sem, vmem_limit_bytes=VMEM_LIMIT)


def _dense_kernel(*refs, rms, swiglu, residual, stage):
    it = iter(refs)
    a_ref = next(it)
    g_ref = next(it) if rms else None
    w_ref = next(it)
    w2_ref = next(it) if swiglu else None
    r_ref = next(it) if residual else None
    o_ref = next(it)
    a_sc = next(it) if stage else None

    if stage:
        @pl.when(pl.program_id(1) == 0)
        def _():
            a = a_ref[...].astype(F32)
            if rms:
                a = a * lax.rsqrt(jnp.mean(a * a, axis=-1, keepdims=True) + NORM_EPS) * g_ref[...]
            a_sc[...] = a.astype(BF16)

        a = a_sc[...]
    else:
        a = a_ref[...]
    y = jnp.dot(a, w_ref[...], preferred_element_type=F32)
    if swiglu:
        u = jnp.dot(a, w2_ref[...], preferred_element_type=F32)
        y = y * _sigmoid(y) * u
    if residual:
        y = y + r_ref[...]
    o_ref[...] = y.astype(o_ref.dtype)


def _dense(a, w, layer, *, tm, tn, g=None, residual=None, swiglu=False, out_dtype=F32, name):
    m, k = a.shape
    n = w.shape[2] // 2 if swiglu else w.shape[2]
    assert m % tm == 0 and n % tn == 0
    rms = g is not None
    stage = rms or a.dtype != BF16
    nj = n // tn
    in_specs = [pl.BlockSpec((tm, k), lambda i, j: (i, 0))]
    args = [a]
    if rms:
        in_specs.append(pl.BlockSpec((1, k), lambda i, j: (0, 0)))
        args.append(g.reshape(1, k))
    in_specs.append(pl.BlockSpec((None, k, tn), lambda i, j: (layer, 0, j)))
    args.append(w)
    if swiglu:
        in_specs.append(pl.BlockSpec((None, k, tn), lambda i, j: (layer, 0, j + nj)))
        args.append(w)
    if residual is not None:
        in_specs.append(pl.BlockSpec((tm, tn), lambda i, j: (i, j)))
        args.append(residual)
    return pl.pallas_call(
        functools.partial(_dense_kernel, rms=rms, swiglu=swiglu, residual=residual is not None, stage=stage),
        grid=(m // tm, nj),
        in_specs=in_specs,
        out_specs=pl.BlockSpec((tm, tn), lambda i, j: (i, j)),
        out_shape=jax.ShapeDtypeStruct((m, n), out_dtype),
        scratch_shapes=[pltpu.VMEM((tm, k), BF16)] if stage else [],
        compiler_params=_cparams("arbitrary", "arbitrary"),
        name=name,
    )(*args)


def _segsum64(x, ones_blk):
    parts = [
        _mm(x[:, LANES * i : LANES * (i + 1)], ones_blk, prec="a") for i in range(x.shape[1] // LANES)
    ]
    return jnp.concatenate(parts, axis=1)


def _head_ones():
    r = _iota2((LANES, LANES), 0) >> _log2(RW_HD)
    c = _iota2((LANES, LANES), 1) >> _log2(RW_HD)
    return _onehot(r == c)


def _rw_prep_kernel(
    pr_ref, pk_ref, pv_ref, pz_ref, sr_ref, sk_ref, sv_ref, sz_ref, vec_ref, muz_ref, lora_ref,
    r_o, k_o, v_o, kk_o, bb_o, lw_o, bonus_o, g_o,
    prev_r, prev_k, prev_v, prev_z,
):
    tb = pr_ref.shape[0]

    @pl.when(pl.program_id(1) == 0)
    def _():
        prev_r[...] = sr_ref[0]
        prev_k[...] = sk_ref[0]
        prev_v[...] = sv_ref[0]
        prev_z[...] = sz_ref[0]

    def shift(p_ref, prev, mu):
        p = p_ref[...]
        row = _iota2(p.shape, 0)
        p_prev = jnp.where(row == 0, prev[...], pltpu.roll(p, 1, 0))
        prev[...] = p[tb - 1 : tb, :]
        return p + mu * (p_prev - p)

    vec = vec_ref[...]
    w0, a0, k_k, k_a, r_k = vec[0:1], vec[1:2], vec[2:3], vec[3:4], vec[4:5]
    r = shift(pr_ref, prev_r, vec[5:6])
    k = shift(pk_ref, prev_k, vec[6:7])
    v = shift(pv_ref, prev_v, vec[7:8])
    z = shift(pz_ref, prev_z, muz_ref[...])

    lane = _iota2(z.shape, 1)
    in_w = lane < RW_DECAY_LORA
    in_a = (lane >= RW_DECAY_LORA) & (lane < RW_DECAY_LORA + RW_A_LORA)
    lora = lora_ref[...]
    lw_lin = _mm(jnp.where(in_w, jnp.tanh(z), 0.0), lora, prec=3)
    a_lin = _mm(jnp.where(in_a, z, 0.0), lora, prec=3)
    g = _mm(jnp.where(in_w | in_a, 0.0, _sigmoid(z)), lora, prec=3)

    lw = -jnp.exp(-_softplus(-(w0 + lw_lin)) - 0.5)
    a = _sigmoid(a0 + a_lin)
    ones_blk = _head_ones()
    kk = k * k_k
    kk = kk / jnp.maximum(jnp.sqrt(_segsum64(kk * kk, ones_blk)), 1e-12)
    k = k * (1.0 + (a - 1.0) * k_a)
    r_o[...] = r
    k_o[...] = k
    v_o[...] = v
    kk_o[...] = kk
    bb_o[...] = a * kk
    lw_o[...] = lw
    bonus_o[...] = _segsum64(r * k * r_k, ones_blk) * v
    g_o[...] = g


def _rw_prep(proj, col0, col_z, shift0, vec, muz, lora, *, nseq, t, tb, name):
    w = 512
    nt = t // tb
    row = lambda b, i: b * nt + i
    sr, sk, sv, sz = shift0
    in_specs = [
        pl.BlockSpec((tb, w), lambda b, i: (row(b, i), col0)),
        pl.BlockSpec((tb, w), lambda b, i: (row(b, i), col0 + 1)),
        pl.BlockSpec((tb, w), lambda b, i: (row(b, i), col0 + 2)),
        pl.BlockSpec((tb, 256), lambda b, i: (row(b, i), col_z)),
        pl.BlockSpec((1, 1, w), lambda b, i: (b, 0, 0)),
        pl.BlockSpec((1, 1, w), lambda b, i: (b, 0, 0)),
        pl.BlockSpec((1, 1, w), lambda b, i: (b, 0, 0)),
        pl.BlockSpec((1, 1, 256), lambda b, i: (b, 0, 0)),
        pl.BlockSpec((8, w), lambda b, i: (0, 0)),
        pl.BlockSpec((1, 256), lambda b, i: (0, 0)),
        pl.BlockSpec((256, w), lambda b, i: (0, 0)),
    ]
    out = jax.ShapeDtypeStruct((nseq * t, w), F32)
    return pl.pallas_call(
        _rw_prep_kernel,
        grid=(nseq, nt),
        in_specs=in_specs,
        out_specs=[pl.BlockSpec((tb, w), lambda b, i: (row(b, i), 0))] * 8,
        out_shape=[out] * 8,
        scratch_shapes=[pltpu.VMEM((1, w), F32)] * 3 + [pltpu.VMEM((1, 256), F32)],
        compiler_params=_cparams("arbitrary", "arbitrary"),
        name=name,
    )(proj, proj, proj, proj, sr, sk, sv, sz, vec, muz, lora)


RW_PREC_SOLVE = 1
RW_PREC_APPLY = 1


def _rw_scan_kernel(
    r_ref, k_ref, v_ref, kk_ref, bb_ref, lw_ref, bonus_ref, g_ref, ln_ref, s0_ref,
    o_ref, st_ref, a_sc, *, c, valid, nchunks,
):
    ci = pl.program_id(1)

    @pl.when(ci == 0)
    def _():
        a_sc[...] = s0_ref[...]

    hp = LANES // RW_HD
    n = hp * c
    ti = _iota2((n, n), 0)
    si = _iota2((n, n), 1)
    same_head = (ti >> _log2(c)) == (si >> _log2(c))
    incl = same_head & (si <= ti)
    strict = same_head & (si < ti)
    tri = _onehot(_iota2((c, c), 1) <= _iota2((c, c), 0))
    ri = _iota2((LANES, LANES), 0)
    cj = _iota2((LANES, LANES), 1)
    eye = ri == cj
    blockdiag = (ri >> _log2(RW_HD)) == (cj >> _log2(RW_HD))
    ones_blk = _onehot(blockdiag)
    own_lane = (_iota2((n, 1), 0) >> _log2(c)) == (_iota2((1, LANES), 1) >> _log2(RW_HD))
    live = None
    if valid < c:
        live = (_iota2((c, 1), 0) + ci * c) < valid
    mms = functools.partial(_mm, prec=RW_PREC_SOLVE)
    mma = functools.partial(_mm, prec=RW_PREC_APPLY)

    def dup(x):
        return jnp.concatenate([x] * hp, axis=0)

    def stack(x):
        return jnp.where(own_lane, dup(x), 0.0)

    def fold(x):
        out = x[0:c]
        for h in range(1, hp):
            out = out + x[h * c : (h + 1) * c]
        return out

    nsb, npair = r_ref.shape[0], r_ref.shape[2] // LANES
    chains = [(s, p) for s in range(nsb) for p in range(npair)]
    tiles = range(len(chains))

    def load(ref, masked):
        out = []
        for s, p in chains:
            x = ref[s, :, LANES * p : LANES * (p + 1)]
            out.append(jnp.where(live, x, 0.0) if (masked and live is not None) else x)
        return out

    r_l, v_l = load(r_ref, False), load(v_ref, False)
    k_l, kk_l, bb_l, lw_l = load(k_ref, True), load(kk_ref, True), load(bb_ref, True), load(lw_ref, True)
    lwc_l = [_mm(tri, lw, prec="b") for lw in lw_l]
    lwe_l = [lwc[c - 1 : c, :] for lwc in lwc_l]
    a_st_l, r_t_l, v_st_l, k_h_l, b_h_l, gram_l = [], [], [], [], [], []
    for p in tiles:
        lwc, lwe = lwc_l[p], lwe_l[p]
        e_neg = jnp.exp(-lwc)
        e_end = jnp.exp(lwe - lwc)
        a_st = stack(kk_l[p] * jnp.exp(lwc - lw_l[p]))
        r_t = r_l[p] * jnp.exp(lwc)
        a_st_l.append(a_st)
        r_t_l.append(r_t)
        v_st_l.append(stack(v_l[p]))
        k_h_l.append(k_l[p] * e_end)
        b_h_l.append(bb_l[p] * e_end)
        gram_l.append(mms(jnp.concatenate([a_st, stack(r_t)], axis=0),
                          jnp.concatenate([dup(bb_l[p] * e_neg), dup(k_l[p] * e_neg)], axis=0), NT))
    l_l = [jnp.where(strict, g[:n, :n], 0.0) for g in gram_l]
    mv_l = [mma(jnp.where(strict, g[:n, n:], 0.0), v_st) for g, v_st in zip(gram_l, v_st_l)]
    t_l = [_onehot(ti == si, F32) for _ in tiles]
    lev = 0
    while (1 << lev) < c:
        off = ((ti >> (lev + 1)) == (si >> (lev + 1))) & ((ti >> lev) != (si >> lev))
        tl_l = [mms(t, jnp.where(off, l, 0.0)) for t, l in zip(t_l, l_l)]
        t_l = [t - mms(tl, t) for t, tl in zip(t_l, tl_l)]
        lev += 1
    z_l = [mma(t, jnp.concatenate([a_st, mv], axis=1)) for t, a_st, mv in zip(t_l, a_st_l, mv_l)]
    qz_l = [mma(jnp.where(incl, g[n:, :n], 0.0), z) for g, z in zip(gram_l, z_l)]
    pv_l = [mma(jnp.where(incl, g[n:, n:], 0.0), v_st) for g, v_st in zip(gram_l, v_st_l)]
    bz_l = [mma(b_h, jnp.concatenate([fold(z[:, :LANES]), fold(z[:, LANES:])], axis=1), TN)
            for b_h, z in zip(b_h_l, z_l)]
    kv_l = [mma(k_h, v, TN) for k_h, v in zip(k_h_l, v_l)]
    ya_l = []
    for p in tiles:
        ry = r_t_l[p] - fold(qz_l[p][:, :LANES])
        g_mat = jnp.where(eye, jnp.exp(lwe_l[p]), 0.0) - jnp.where(blockdiag, bz_l[p][:, :LANES], 0.0)
        ya_l.append(mma(jnp.concatenate([ry, g_mat], axis=0), a_sc[chains[p]]))
    ys = [ya[:c] + fold(pv - qz[:, LANES:]) for ya, pv, qz in zip(ya_l, pv_l, qz_l)]
    states = [ya[c:] + jnp.where(blockdiag, kv - bz[:, LANES:], 0.0) for ya, kv, bz in zip(ya_l, kv_l, bz_l)]
    a_sc[...] = jnp.stack(states).reshape(a_sc.shape)
    y = jnp.concatenate([jnp.concatenate(ys[s * npair : (s + 1) * npair], axis=1) for s in range(nsb)], axis=0)
    mean = _segsum64(y, ones_blk) * (1.0 / RW_HD)
    d = y - mean
    var = _segsum64(d * d, ones_blk) * (1.0 / RW_HD)
    yn = (d * lax.rsqrt(var + RW_GN_EPS)).reshape(o_ref.shape)
    o_ref[...] = (yn * ln_ref[0:1, :] + ln_ref[1:2, :] + bonus_ref[...]) * g_ref[...]

    @pl.when(ci == nchunks - 1)
    def _():
        st_ref[...] = a_sc[...]


def _rw_scan(prep, ln, s0, *, nseq, t, c, valid, name):
    w = 512
    nchunks = -(-valid // c)
    npair = w // LANES
    nsb = RW_SEQS_PER_STEP if nseq % RW_SEQS_PER_STEP == 0 else 1
    blk = pl.BlockSpec((nsb, c, w), lambda b, i: (b, i, 0))
    st_spec = pl.BlockSpec((nsb, npair, LANES, LANES), lambda b, i: (b, 0, 0, 0))
    o, st = pl.pallas_call(
        functools.partial(_rw_scan_kernel, c=c, valid=valid, nchunks=nchunks),
        grid=(nseq // nsb, nchunks),
        in_specs=[blk] * 8 + [pl.BlockSpec((2, w), lambda b, i: (0, 0)), st_spec],
        out_specs=[blk, st_spec],
        out_shape=[
            jax.ShapeDtypeStruct((nseq, nchunks * c, w), F32),
            jax.ShapeDtypeStruct((nseq, npair, LANES, LANES), F32),
        ],
        scratch_shapes=[pltpu.VMEM((nsb, npair, LANES, LANES), F32)],
        compiler_params=_cparams("arbitrary", "arbitrary"),
        name=name,
    )(*[a.reshape(nseq, t, w) for a in prep], ln, s0)
    return o.reshape(nseq * nchunks * c, w), st


GL_PREC = 1


def _gla_kernel(*refs, tb, valid, nblocks, dk, nsb):
    seq_refs = [refs[5 * s : 5 * s + 5] for s in range(nsb)]
    ab_ref, abias_ref, ng_ref, s0_ref, o_ref, st_ref, s_sc = refs[5 * nsb :]
    bi = pl.program_id(1)

    @pl.when(bi == 0)
    def _():
        s_sc[...] = s0_ref[...]

    mm = functools.partial(_mm, prec=GL_PREC)
    ti = _iota2((tb, tb), 0)
    si = _iota2((tb, tb), 1)
    tri = _onehot(si <= ti)
    trow = _iota2((tb, 1), 0)
    lane_head = _iota2((1, LANES), 1) >> _log2(dk)
    heads_per_tile = LANES // dk
    own_lane = (_iota2((heads_per_tile * tb, 1), 0) >> _log2(tb)) == lane_head

    def dup(x):
        return jnp.concatenate([x] * heads_per_tile, axis=0)

    def stack(x):
        return jnp.where(own_lane, dup(x), 0.0)

    npair = seq_refs[0][0].shape[1] // LANES
    chains = [(s, p) for s in range(nsb) for p in range(npair)]
    q_l, k_l, la_l = [], [], []
    for s in range(nsb):
        q_ref, k_ref, _, xa_ref, _ = seq_refs[s]
        la_all = -_softplus(-(_mm(xa_ref[...], ab_ref[...], prec=3) + abias_ref[...])) * (1.0 / GL_TAU)
        k_all = k_ref[...]
        if valid < tb:
            live = (trow + bi * tb) < valid
            la_all = jnp.where(live, la_all, 0.0)
            k_all = jnp.where(live, k_all, 0.0)
        for p in range(npair):
            sl = slice(LANES * p, LANES * (p + 1))
            q_l.append(q_ref[:, sl] * (dk**-0.5))
            k_l.append(k_all[:, sl])
            la_l.append(la_all[:, sl])
    b_l = [_mm(tri, la, prec="b") for la in la_l]
    att_l = [jnp.where(dup(ti == si), mm(stack(q), k, NT), 0.0) for q, k in zip(q_l, k_l)]
    lev = 1
    while (1 << lev) <= tb:
        half = 1 << (lev - 1)
        sel = _onehot(si == ((ti >> lev) << lev) + (half - 1))
        upper = ((trow >> (lev - 1)) & 1) == 1
        same = dup((ti >> lev) == (si >> lev))
        bref_l = [_mm(sel, b, prec="b") for b in b_l]
        qs_l = [stack(jnp.where(upper, q * jnp.exp(jnp.minimum(b - br, 0.0)), 0.0)) for q, b, br in zip(q_l, b_l, bref_l)]
        ks_l = [jnp.where(upper, 0.0, k * jnp.exp(jnp.minimum(br - b, 0.0))) for k, b, br in zip(k_l, b_l, bref_l)]
        att_l = [att + jnp.where(same, mm(qs, ks, NT), 0.0) for att, qs, ks in zip(att_l, qs_l, ks_l)]
        lev += 1

    s_old_l = [s_sc[s, p] for s, p in chains]
    inter_l = [mm(stack(q * jnp.exp(b)), s_old, NT) for q, b, s_old in zip(q_l, b_l, s_old_l)]
    outs, states = [], []
    for i, (s, p) in enumerate(chains):
        b_end = b_l[i][tb - 1 : tb, :]
        ke = k_l[i] * jnp.exp(b_end - b_l[i])
        s_new = s_old_l[i] * jnp.exp(b_end)
        for h in range(heads_per_tile):
            hv = heads_per_tile * p + h
            v = seq_refs[s][2][:, LANES * hv : LANES * (hv + 1)]
            o = mm(att_l[i][h * tb : (h + 1) * tb], v) + inter_l[i][h * tb : (h + 1) * tb]
            s_new = s_new + mm(v, jnp.where(lane_head == h, ke, 0.0), TN)
            outs.append(o * lax.rsqrt(jnp.mean(o * o, axis=-1, keepdims=True) + NORM_EPS) * ng_ref[...])
        states.append(s_new)

    s_sc[...] = jnp.stack(states).reshape(s_sc.shape)
    per_seq = len(outs) // nsb
    for s in range(nsb):
        gg = seq_refs[s][4][...]
        o_ref[s] = jnp.concatenate(outs[s * per_seq : (s + 1) * per_seq], axis=1) * (gg * _sigmoid(gg))

    @pl.when(bi == nblocks - 1)
    def _():
        st_ref[...] = s_sc[...]


def _gla(proj, cols, ab, abias, ng, s0, *, nseq, t, tb, valid, name):
    cq, ck, cv, cxa, cgg = cols
    nb = t // tb
    nrun = -(-valid // tb)
    npair = 256 // LANES
    nsb = GL_SEQS_PER_STEP if nseq % GL_SEQS_PER_STEP == 0 else 1
    st_spec = pl.BlockSpec((nsb, npair, LANES, LANES), lambda b, i: (b, 0, 0, 0))

    def seq_specs(s):
        row = lambda b, i: (b * nsb + s) * nb + i
        return [
            pl.BlockSpec((tb, 256), lambda b, i: (row(b, i), cq)),
            pl.BlockSpec((tb, 256), lambda b, i: (row(b, i), ck)),
            pl.BlockSpec((tb, 512), lambda b, i: (row(b, i), cv)),
            pl.BlockSpec((tb, LANES), lambda b, i: (row(b, i), cxa)),
            pl.BlockSpec((tb, 512), lambda b, i: (row(b, i), cgg)),
        ]

    o, st = pl.pallas_call(
        functools.partial(_gla_kernel, tb=tb, valid=valid, nblocks=nrun, dk=256 // GL_HEADS, nsb=nsb),
        grid=(nseq // nsb, nrun),
        in_specs=[spec for s in range(nsb) for spec in seq_specs(s)]
        + [
            pl.BlockSpec((LANES, 256), lambda b, i: (0, 0)),
            pl.BlockSpec((1, 256), lambda b, i: (0, 0)),
            pl.BlockSpec((1, LANES), lambda b, i: (0, 0)),
            st_spec,
        ],
        out_specs=[pl.BlockSpec((nsb, tb, 512), lambda b, i: (b, i, 0)), st_spec],
        out_shape=[
            jax.ShapeDtypeStruct((nseq, nrun * tb, 512), F32),
            jax.ShapeDtypeStruct((nseq, npair, LANES, LANES), F32),
        ],
        scratch_shapes=[pltpu.VMEM((nsb, npair, LANES, LANES), F32)],
        compiler_params=_cparams("arbitrary", "arbitrary"),
        name=name,
    )(*([proj] * (5 * nsb)), ab, abias, ng, s0)
    return o.reshape(nseq * nrun * tb, 512), st


def _fx_prep_kernel(*refs, stacked):
    q_ref, k_ref, v_ref, fl_ref, qg_ref, kg_ref, bf_ref = refs[:7]
    qb_o, kf_o, vf_o, kb_o, vb_o, lf_o, c_o, carry = refs[9:] if stacked else refs[7:]
    tb = q_ref.shape[0]

    @pl.when(pl.program_id(1) == 0)
    def _():
        carry[...] = jnp.zeros_like(carry)

    def headnorm(x, gain):
        return x * lax.rsqrt(jnp.mean(x * x, axis=-1, keepdims=True) + NORM_EPS) * gain

    nh = q_ref.shape[1] // FX_HD
    for h in range(nh):
        sl = slice(FX_HD * h, FX_HD * (h + 1))
        qb_o[:, sl] = (headnorm(q_ref[:, sl], qg_ref[...]) * (FX_HD**-0.5 * LOG2E)).astype(BF16)
        kn = headnorm(k_ref[:, sl], kg_ref[...])
        kb_o[:, sl] = kn.astype(BF16)
        kf_o[pl.ds(h, tb, stride=nh), :] = kn
        vf_o[pl.ds(h, tb, stride=nh), :] = v_ref[:, sl]
    vb_o[...] = v_ref[...].astype(BF16)
    logf = -_softplus(-(fl_ref[...] + bf_ref[...]))
    lf_o[...] = logf
    tri = _onehot(_iota2((tb, tb), 1) <= _iota2((tb, tb), 0))
    c = _mm(tri, logf, prec="b") + carry[...]
    c_o[...] = c
    carry[...] = c[tb - 1 : tb, :]


def _fx_prep(proj, cols, qg, kg, bf, *, nseq, t, tb, name, stack=None):
    cq, ck, cv, cfl = cols
    nt = t // tb
    row = lambda b, i: b * nt + i
    w = 1024
    nh = w // FX_HD
    wide = pl.BlockSpec((tb, w), lambda b, i: (row(b, i), 0))
    thin = pl.BlockSpec((tb, LANES), lambda b, i: (row(b, i), 0))
    vecspec = pl.BlockSpec((1, LANES), lambda b, i: (0, 0))
    rows = nseq * t
    extra_in, extra_args, aliases = [], [], {}
    kv_spec = pl.BlockSpec((tb * nh, FX_HD), lambda b, i: (row(b, i), 0))
    kv_shape = jax.ShapeDtypeStruct((rows * nh, FX_HD), F32)
    if stack is not None:
        k_all, v_all, layer = stack
        extra_in = [pl.BlockSpec(memory_space=pl.ANY)] * 2
        extra_args = [k_all, v_all]
        aliases = {7: 1, 8: 2}
        kv_spec = pl.BlockSpec((None, None, tb * nh, FX_HD), lambda b, i: (b, layer, i, 0))
        kv_shape = jax.ShapeDtypeStruct(k_all.shape, F32)
    return pl.pallas_call(
        functools.partial(_fx_prep_kernel, stacked=stack is not None),
        grid=(nseq, nt),
        in_specs=[
            pl.BlockSpec((tb, w), lambda b, i: (row(b, i), cq)),
            pl.BlockSpec((tb, w), lambda b, i: (row(b, i), ck)),
            pl.BlockSpec((tb, w), lambda b, i: (row(b, i), cv)),
            pl.BlockSpec((tb, LANES), lambda b, i: (row(b, i), cfl)),
            vecspec, vecspec, vecspec,
        ] + extra_in,
        out_specs=[wide, kv_spec, kv_spec, wide, wide, thin, thin],
        out_shape=[
            jax.ShapeDtypeStruct((rows, w), BF16),
            kv_shape,
            kv_shape,
            jax.ShapeDtypeStruct((rows, w), BF16),
            jax.ShapeDtypeStruct((rows, w), BF16),
            jax.ShapeDtypeStruct((rows, LANES), F32),
            jax.ShapeDtypeStruct((rows, LANES), F32),
        ],
        scratch_shapes=[pltpu.VMEM((1, LANES), F32)],
        input_output_aliases=aliases,
        compiler_params=_cparams("arbitrary", "arbitrary"),
        name=name,
    )(proj, proj, proj, proj, qg, kg, bf, *extra_args)


def _with_ones(v):
    return jnp.concatenate([v, jnp.ones_like(v)], axis=1)


def _flash_kernel(q_ref, k_ref, v_ref, c_ref, cref_ref, o_ref, m_sc, acc_sc, *, tq, tk, hb):
    qi = pl.program_id(2)
    kj = pl.program_id(3)

    @pl.when(kj == 0)
    def _():
        m_sc[...] = jnp.full_like(m_sc, -jnp.inf)
        acc_sc[...] = jnp.zeros_like(acc_sc)

    def update(masked):
        heads = range(hb)
        sl = [slice(FX_HD * h, FX_HD * (h + 1)) for h in heads]
        s_l = [lax.dot_general(q_ref[:, sl[h]], k_ref[:, sl[h]], NT, preferred_element_type=F32) for h in heads]
        s_l = [s_l[h] + (cref_ref[0, h, :, 0:1] - c_ref[0, h]) * LOG2E for h in heads]
        if masked:
            keep = (kj * tk + _iota2((tq, tk), 1)) <= (qi * tq + _iota2((tq, tk), 0))
            s_l = [jnp.where(keep, s, -jnp.inf) for s in s_l]
        m_old = [m_sc[h] for h in heads]
        m_new = [jnp.maximum(m_old[h], jnp.max(s_l[h], axis=-1, keepdims=True)) for h in heads]
        p_l = [jnp.exp2(s_l[h] - pltpu.repeat(m_new[h], tk // LANES, axis=1)) for h in heads]
        pv_l = [jnp.dot(p_l[h].astype(BF16), _with_ones(v_ref[:, sl[h]]), preferred_element_type=F32) for h in heads]
        for h in heads:
            alpha = jnp.exp2(m_old[h] - m_new[h])
            acc_sc[h] = pltpu.repeat(alpha, 2, axis=1) * acc_sc[h] + pv_l[h]
            m_sc[h] = m_new[h]

    first_key, last_key = kj * tk, kj * tk + (tk - 1)
    first_query, last_query = qi * tq, qi * tq + (tq - 1)

    @pl.when(last_key <= first_query)
    def _():
        update(False)

    @pl.when((last_key > first_query) & (first_key <= last_query))
    def _():
        update(True)

    @pl.when(kj == pl.num_programs(3) - 1)
    def _():
        for h in range(hb):
            acc = acc_sc[h]
            o_ref[:, FX_HD * h : FX_HD * (h + 1)] = acc[:, :FX_HD] / acc[:, FX_HD:]


def _flash(qb, kb, vb, c_row, *, nseq, t, tq, tk, name):
    nh = qb.shape[1] // FX_HD
    hb = FLASH_HEADS_PER_STEP if nh % FLASH_HEADS_PER_STEP == 0 else 1
    nq, nk = t // tq, t // tk
    last = lambda i: (i * tq + tq - 1) // tk
    w = hb * FX_HD
    return pl.pallas_call(
        functools.partial(_flash_kernel, tq=tq, tk=tk, hb=hb),
        grid=(nseq, nh // hb, nq, nk),
        in_specs=[
            pl.BlockSpec((tq, w), lambda b, h, i, j: (b * nq + i, h)),
            pl.BlockSpec((tk, w), lambda b, h, i, j: (b * nk + jnp.minimum(j, last(i)), h)),
            pl.BlockSpec((tk, w), lambda b, h, i, j: (b * nk + jnp.minimum(j, last(i)), h)),
            pl.BlockSpec((1, hb, 1, tk), lambda b, h, i, j: (b, h, 0, jnp.minimum(j, last(i)))),
            pl.BlockSpec((1, hb, 1, LANES), lambda b, h, i, j: (b, h, 0, i * (tq // LANES))),
        ],
        out_specs=pl.BlockSpec((tq, w), lambda b, h, i, j: (b * nq + i, h)),
        out_shape=jax.ShapeDtypeStruct((nseq * t, nh * FX_HD), F32),
        scratch_shapes=[pltpu.VMEM((hb, tq, LANES), F32), pltpu.VMEM((hb, tq, 2 * FX_HD), F32)],
        compiler_params=_cparams("arbitrary", "arbitrary", "arbitrary", "arbitrary"),
        name=name,
    )(qb, kb, vb, c_row, c_row)


def _decode_kernel(pt_ref, q_ref, kn_ref, vn_ref, cn_ref, *rest, nq, nh, pp):
    kc_refs, vc_refs, lf_refs = rest[0:pp], rest[pp : 2 * pp], rest[2 * pp : 3 * pp]
    o_ref, m_sc, acc_sc, carry = rest[3 * pp :]
    j = pl.program_id(1)
    heads = range(nh)
    q_l = [q_ref[:, FX_HD * h : FX_HD * (h + 1)] for h in heads]

    def attend(pages, causal):
        kv = [[load(h) for h in heads] for load, _ in pages]
        s = [[lax.dot_general(q_l[h], kv[i][h][0], NT, preferred_element_type=F32) + pages[i][1][h : h + 1, :]
              for h in heads] for i in range(len(pages))]
        if causal:
            keep = _iota2((nq, PAGE), 1) <= _iota2((nq, PAGE), 0)
            s = [[jnp.where(keep, x, -jnp.inf) for x in row] for row in s]
        for h in heads:
            m_old = m_sc[h]
            m_new = m_old
            for i in range(len(pages)):
                m_new = jnp.maximum(m_new, jnp.max(s[i][h], axis=-1, keepdims=True))
            acc = pltpu.repeat(jnp.exp2(m_old - m_new), 2, axis=1) * acc_sc[h]
            for i in range(len(pages)):
                p = jnp.exp2(s[i][h] - m_new)
                acc = acc + jnp.dot(p.astype(BF16), _with_ones(kv[i][h][1]), preferred_element_type=F32)
            acc_sc[h] = acc
            m_sc[h] = m_new

    @pl.when(j == 0)
    def _():
        m_sc[...] = jnp.full_like(m_sc, -jnp.inf)
        acc_sc[...] = jnp.zeros_like(acc_sc)
        carry[...] = jnp.zeros_like(carry)

        def new_rows(h):
            sl = slice(FX_HD * h, FX_HD * (h + 1))
            return kn_ref[:, sl], vn_ref[:, sl]

        attend([(new_rows, -cn_ref[0] * LOG2E)], True)

    @pl.when(j > 0)
    def _():
        after = _onehot(_iota2((PAGE, PAGE), 0) > _iota2((PAGE, PAGE), 1))
        both = jnp.concatenate([after, jnp.ones((PAGE, PAGE), BF16)], axis=1)
        run = carry[...]
        pages = []
        for i in range(pp):

            def cached_rows(h, i=i):
                rows = pl.ds(h, PAGE, stride=nh)
                return kc_refs[i][rows, :].astype(BF16), vc_refs[i][rows, :].astype(BF16)

            sums = _mm(lf_refs[i][...], both, TN, prec="a")
            pages.append((cached_rows, (run + sums[:, :PAGE]) * LOG2E))
            run = run + sums[:, PAGE:]
        carry[...] = run
        attend(pages, False)

    @pl.when(j == pl.num_programs(1) - 1)
    def _():
        for h in heads:
            acc = acc_sc[h]
            o_ref[:, FX_HD * h : FX_HD * (h + 1)] = acc[:, :FX_HD] / acc[:, FX_HD:]


def _decode(qb, kb, vb, cn_t, cache_k, cache_v, cache_logf, page_table, layer, *, nq, name):
    nseq, npages = page_table.shape
    pool, depth, _, nh, hd = cache_k.shape
    w = nh * hd
    pp = DEC_PAGES_PER_STEP if npages % DEC_PAGES_PER_STEP == 0 else 1
    ck = cache_k.reshape(pool, depth, PAGE * nh, hd)
    cv = cache_v.reshape(pool, depth, PAGE * nh, hd)

    def page_of(i):
        def index(b, j, pt):
            back = (jnp.maximum(j, 1) - 1) * pp + i
            return (pt[b * npages + (npages - 1 - back)], layer, 0, 0)

        return index

    grid_spec = pltpu.PrefetchScalarGridSpec(
        num_scalar_prefetch=1,
        grid=(nseq, 1 + npages // pp),
        in_specs=[
            pl.BlockSpec((nq, w), lambda b, j, pt: (b * (PAGE // nq), 0)),
            pl.BlockSpec((PAGE, w), lambda b, j, pt: (b, 0)),
            pl.BlockSpec((PAGE, w), lambda b, j, pt: (b, 0)),
            pl.BlockSpec((1, nh, PAGE), lambda b, j, pt: (b, 0, 0)),
        ]
        + [pl.BlockSpec((None, None, PAGE * nh, hd), page_of(i)) for i in range(pp)] * 2
        + [pl.BlockSpec((None, None, PAGE, nh), page_of(i)) for i in range(pp)],
        out_specs=pl.BlockSpec((nq, w), lambda b, j, pt: (b, 0)),
        scratch_shapes=[
            pltpu.VMEM((nh, nq, LANES), F32),
            pltpu.VMEM((nh, nq, 2 * hd), F32),
            pltpu.VMEM((nh, PAGE), F32),
        ],
    )
    return pl.pallas_call(
        functools.partial(_decode_kernel, nq=nq, nh=nh, pp=pp),
        grid_spec=grid_spec,
        out_shape=jax.ShapeDtypeStruct((nseq * nq, w), F32),
        compiler_params=_cparams("arbitrary", "arbitrary"),
        name=name,
    )(page_table.reshape(-1), qb, kb, vb, cn_t, *([ck] * pp), *([cv] * pp), *([cache_logf] * pp))


RW_CHUNK = 64
RW_SEQS_PER_STEP = 2
GL_BLOCK = 128
GL_SEQS_PER_STEP = 2
FLASH_HEADS_PER_STEP = 2
DEC_Q_ROWS = 16
DEC_PAGES_PER_STEP = 8
TM = 768
COL_FX_Q, COL_FX_K, COL_FX_V = 0, 1024, 2048
COL_RW_R = 3072
COL_GL_V, COL_GL_GG = 4608, 5120
COL_GL_Q, COL_GL_K = 5632, 5888
COL_RW_Z = 6144
COL_FX_FL = 6400
COL_GL_XA = 6528
COLS_PAD = 6656


def _pad_cols(x, width):
    return jnp.pad(x, [(0, 0)] * (x.ndim - 1) + [(0, width - x.shape[-1])])


def _relayout_in(w, rw_w, fx_w, gl_kw, gl_vw, nh_fx):
    rw_z = RW_DECAY_LORA + RW_A_LORA + RW_G_LORA
    o = 0
    rw_rkv = w[..., o : o + 3 * rw_w]; o += 3 * rw_w
    rw_zz = w[..., o : o + rw_z]; o += rw_z
    fx_qkv = w[..., o : o + 3 * fx_w]; o += 3 * fx_w
    fx_fl = w[..., o : o + nh_fx]; o += nh_fx
    gl_qk = w[..., o : o + 2 * gl_kw]; o += 2 * gl_kw
    gl_v = w[..., o : o + gl_vw]; o += gl_vw
    gl_xa = w[..., o : o + GL_GATE_LORA]; o += GL_GATE_LORA
    gl_gg = w[..., o : o + gl_vw]; o += gl_vw
    assert o == w.shape[-1]
    return jnp.concatenate(
        [fx_qkv, rw_rkv, gl_v, gl_gg, gl_qk, _pad_cols(rw_zz, 256), _pad_cols(fx_fl, LANES), _pad_cols(gl_xa, LANES)],
        axis=-1,
    )


def _mixers(proj, layer, lp, states, paged, *, nseq, t, valid, tag, kv_stack=None):
    (shift0, rw_s0, gl_s0) = states
    tb = min(t, 256)
    prep = _rw_prep(proj, COL_RW_R // 512, COL_RW_Z // 256, shift0, lp["rw_vec"], lp["rw_muz"], lp["rw_lora"],
                    nseq=nseq, t=t, tb=tb, name=f"rw_prep_{tag}")
    o_rw, rw_state = _rw_scan(prep, lp["rw_ln"], rw_s0, nseq=nseq, t=t, c=min(t, RW_CHUNK), valid=valid,
                              name=f"rw_scan_{tag}")
    o_gl, gl_state = _gla(
        proj, (COL_GL_Q // 256, COL_GL_K // 256, COL_GL_V // 512, COL_GL_XA // LANES, COL_GL_GG // 512),
        lp["gl_ab"], lp["gl_abias"], lp["gl_ng"], gl_s0, nseq=nseq, t=t, tb=min(t, GL_BLOCK), valid=valid,
        name=f"gla_{tag}")
    qb, kf, vf, kb, vb, logf, c = _fx_prep(
        proj, (COL_FX_Q // 1024, COL_FX_K // 1024, COL_FX_V // 1024, COL_FX_FL // LANES),
        lp["fx_qg"], lp["fx_kg"], lp["fx_bf"], nseq=nseq, t=t, tb=tb, name=f"fx_prep_{tag}",
        stack=None if kv_stack is None else (*kv_stack, layer))
    nh = kb.shape[1] // FX_HD
    c_t = jnp.transpose(c.reshape(nseq, t, LANES)[:, :, :nh], (0, 2, 1))
    if paged is None:
        blk = min(t, 512)
        o_fx = _flash(qb, kb, vb, c_t[:, :, None, :], nseq=nseq, t=t, tq=blk, tk=blk, name=f"flash_{tag}")
    else:
        cache_k, cache_v, cache_logf, page_table = paged
        nq = DEC_Q_ROWS
        assert t == PAGE and valid <= nq
        o_fx = _decode(qb, kb, vb, c_t, cache_k, cache_v, cache_logf, page_table, layer, nq=nq, name=f"decode_{tag}")
        o_fx = o_fx.reshape(nseq, nq, -1)[:, :valid].reshape(nseq * valid, -1)
    return o_rw, o_fx, o_gl, kf, logf, rw_state, gl_state, vf


def kernel(x_prompt, x_sample, cache_k, cache_v, cache_logf, state_rwkv, state_rwkv_shift, state_gla,
           page_table, attn_norm_g, w_in, rw_mu, rw_w_b, rw_w0, rw_a_b, rw_a0, rw_g_b, rw_k_k, rw_k_a,
           rw_r_k, rw_lnx_g, rw_lnx_b, fx_qn_g, fx_kn_g, fx_b_f, gl_alpha_b, gl_alpha_bias, gl_norm_g,
           w_out, ffn_norm_g, w_gu, w_down):
    nb, seq, d_model = x_prompt.shape
    ns, dseq, _ = x_sample.shape
    depth = w_in.shape[0]
    rw_w = rw_w0.shape[1]
    rw_heads = rw_w // RW_HD
    nh_fx = fx_b_f.shape[1]
    fx_w = nh_fx * FX_HD
    gl_kw = gl_alpha_b.shape[2]
    gl_dk = gl_kw // GL_HEADS
    gl_dv = gl_norm_g.shape[1]
    gl_vw = GL_HEADS * gl_dv
    rw_cols = 3 * rw_w + RW_DECAY_LORA + RW_A_LORA + RW_G_LORA
    assert (rw_w, fx_w, gl_kw, gl_vw, FX_HD, gl_dv) == (512, 1024, 256, 512, 128, 128)
    n_prompt = nb * seq
    n_sample = ns * dseq
    rows = n_prompt + n_sample
    tm = TM if rows >= TM else 64
    rows_pad = -(-rows // tm) * tm
    relayout = functools.partial(_relayout_in, rw_w=rw_w, fx_w=fx_w, gl_kw=gl_kw, gl_vw=gl_vw, nh_fx=nh_fx)

    w_in_p = relayout(w_in).astype(BF16)
    w_out_b = w_out.astype(BF16)
    w_gu_b = w_gu.astype(BF16)
    w_down_b = w_down.astype(BF16)
    rw_mu_r, rw_mu_k, rw_mu_v, rw_mu_z = (
        rw_mu[:, :rw_w], rw_mu[:, rw_w : 2 * rw_w], rw_mu[:, 2 * rw_w : 3 * rw_w], rw_mu[:, 3 * rw_w :])
    layers = []
    for l in range(depth):
        layers.append(dict(
            rw_vec=jnp.stack([rw_w0[l], rw_a0[l], rw_k_k[l], rw_k_a[l], rw_r_k[l].reshape(-1),
                              rw_mu_r[l], rw_mu_k[l], rw_mu_v[l]]),
            rw_muz=_pad_cols(rw_mu_z[l][None], 256),
            rw_lora=jnp.pad(jnp.concatenate([rw_w_b[l], rw_a_b[l], rw_g_b[l]], axis=0),
                            ((0, 256 - (rw_cols - 3 * rw_w)), (0, 0))),
            rw_ln=jnp.stack([rw_lnx_g[l], rw_lnx_b[l]]),
            fx_qg=fx_qn_g[l][None], fx_kg=fx_kn_g[l][None], fx_bf=_pad_cols(fx_b_f[l][None], LANES),
            gl_ab=jnp.pad(gl_alpha_b[l], ((0, LANES - GL_GATE_LORA), (0, 0))),
            gl_abias=gl_alpha_bias[l][None], gl_ng=gl_norm_g[l][None],
        ))

    def rw_state_in(s):
        n = s.shape[0]
        a = jnp.swapaxes(s, -1, -2).reshape(n, rw_heads // 2, 2, RW_HD, RW_HD)
        z = jnp.zeros_like(a[:, :, 0])
        top = jnp.concatenate([a[:, :, 0], z], axis=-1)
        bot = jnp.concatenate([z, a[:, :, 1]], axis=-1)
        return jnp.concatenate([top, bot], axis=-2)

    def rw_state_out(a):
        n = a.shape[0]
        h0 = a[:, :, :RW_HD, :RW_HD]
        h1 = a[:, :, RW_HD:, RW_HD:]
        return jnp.swapaxes(jnp.stack([h0, h1], axis=2).reshape(n, rw_heads, RW_HD, RW_HD), -1, -2)

    def gl_state_in(s):
        n = s.shape[0]
        a = jnp.swapaxes(s, -1, -2).reshape(n, GL_HEADS // 2, 2, gl_dv, gl_dk)
        return jnp.transpose(a, (0, 1, 3, 2, 4)).reshape(n, GL_HEADS // 2, gl_dv, 2 * gl_dk)

    def gl_state_out(a):
        n = a.shape[0]
        a = jnp.transpose(a.reshape(n, GL_HEADS // 2, gl_dv, 2, gl_dk), (0, 1, 3, 2, 4))
        return jnp.swapaxes(a.reshape(n, GL_HEADS, gl_dv, gl_dk), -1, -2)

    def shift_in(s):
        return (s[:, None, :rw_w], s[:, None, rw_w : 2 * rw_w], s[:, None, 2 * rw_w : 3 * rw_w],
                _pad_cols(s[:, None, 3 * rw_w :], 256))

    def shift_out(proj_rows):
        return jnp.concatenate([proj_rows[:, COL_RW_R : COL_RW_R + 3 * rw_w],
                                proj_rows[:, COL_RW_Z : COL_RW_Z + rw_cols - 3 * rw_w]], axis=-1)

    x = jnp.concatenate([x_prompt.reshape(n_prompt, d_model), x_sample.reshape(n_sample, d_model),
                         jnp.zeros((rows_pad - rows, d_model), x_prompt.dtype)], axis=0)
    t_s = PAGE
    p_states = (shift_in(jnp.zeros((nb, rw_cols), F32)), jnp.zeros((nb, rw_heads // 2, LANES, LANES), F32),
                jnp.zeros((nb, GL_HEADS // 2, LANES, LANES), F32))
    outs = {k: [] for k in ("plf", "prw", "psh", "pgl", "sk", "sv", "slf", "srw", "ssh", "sgl")}
    pk_all = jnp.zeros((nb, depth, seq * nh_fx, FX_HD), F32)
    pv_all = jnp.zeros((nb, depth, seq * nh_fx, FX_HD), F32)
    for l in range(depth):
        lp = layers[l]
        proj = _dense(x, w_in_p, l, tm=tm, tn=512, g=attn_norm_g[l], name=f"w_in_{l}")
        proj_s = jnp.pad(proj[n_prompt:rows].reshape(ns, dseq, COLS_PAD), ((0, 0), (0, t_s - dseq), (0, 0)))
        proj_s = proj_s.reshape(ns * t_s, COLS_PAD)
        s_states = (shift_in(state_rwkv_shift[:, l]), rw_state_in(state_rwkv[:, l]), gl_state_in(state_gla[:, l]))
        po = _mixers(proj, l, lp, p_states, None, nseq=nb, t=seq, valid=seq, tag=f"p{l}", kv_stack=(pk_all, pv_all))
        pk_all, pv_all = po[3], po[7]
        so = _mixers(proj_s, l, lp, s_states, (cache_k, cache_v, cache_logf, page_table),
                     nseq=ns, t=t_s, valid=dseq, tag=f"s{l}")

        def unpad(a):
            return a.reshape(ns, a.shape[0] // ns, -1)[:, :dseq].reshape(n_sample, -1)

        mix = jnp.concatenate([
            jnp.concatenate([po[0], po[1], po[2]], axis=1),
            jnp.concatenate([unpad(so[0]), so[1], unpad(so[2])], axis=1),
            jnp.zeros((rows_pad - rows, d_model), F32)], axis=0)
        x = _dense(mix, w_out_b, l, tm=tm, tn=512, residual=x, name=f"w_out_{l}")
        act = _dense(x, w_gu_b, l, tm=tm, tn=512, g=ffn_norm_g[l], swiglu=True, out_dtype=BF16, name=f"w_gu_{l}")
        x = _dense(act, w_down_b, l, tm=tm, tn=256, residual=x, name=f"w_down_{l}")

        outs["plf"].append(po[4][:, :nh_fx].reshape(nb, seq, nh_fx))
        outs["prw"].append(rw_state_out(po[5]))
        outs["psh"].append(shift_out(proj[:n_prompt].reshape(nb, seq, COLS_PAD)[:, seq - 1]))
        outs["pgl"].append(gl_state_out(po[6]))
        outs["sk"].append(so[3].reshape(ns, t_s, nh_fx, FX_HD)[:, :dseq])
        outs["sv"].append(so[7].reshape(ns, t_s, nh_fx, FX_HD)[:, :dseq])
        outs["slf"].append(unpad(so[4])[:, :nh_fx].reshape(ns, dseq, nh_fx))
        outs["srw"].append(rw_state_out(so[5]))
        outs["ssh"].append(shift_out(proj[n_prompt:rows].reshape(ns, dseq, COLS_PAD)[:, dseq - 1]))
        outs["sgl"].append(gl_state_out(so[6]))

    st = {k: jnp.stack(v, axis=1) for k, v in outs.items()}
    y_prompt = x[:n_prompt].reshape(nb, seq, d_model)
    y_sample = x[n_prompt:rows].reshape(ns, dseq, d_model)
    kv_shape = (nb, depth, seq, nh_fx, FX_HD)
    return (y_prompt, y_sample, pk_all.reshape(kv_shape), pv_all.reshape(kv_shape), st["plf"], st["prw"], st["psh"], st["pgl"],
            st["sk"], st["sv"], st["slf"], st["srw"], st["ssh"], st["sgl"])
```

```python
import functools

import jax
import jax.numpy as jnp
from jax import lax
from jax.experimental import pallas as pl
from jax.experimental.pallas import tpu as pltpu

F32 = jnp.float32
BF16 = jnp.bfloat16

RW_HD = 64
RW_DECAY_LORA = 32
RW_A_LORA = 32
RW_G_LORA = 96
RW_GN_EPS = 64e-5
FX_HD = 128
GL_HEADS = 4
GL_GATE_LORA = 16
GL_TAU = 16.0
NORM_EPS = 1e-6
PAGE = 128
LOG2E = 1.4426950408889634

LANES = 128
VMEM_LIMIT = 56 * 1024 * 1024

NN = (((1,), (0,)), ((), ()))
NT = (((1,), (1,)), ((), ()))
TN = (((0,), (0,)), ((), ()))


def _split2(x):
    hi = x.astype(BF16)
    lo = (x - hi.astype(F32)).astype(BF16)
    return hi, lo


def _split3(x):
    hi = x.astype(BF16)
    r = x - hi.astype(F32)
    mid = r.astype(BF16)
    lo = (r - mid.astype(F32)).astype(BF16)
    return hi, mid, lo


def _mm(a, b, dims=NN, prec=1):
    dot = functools.partial(lax.dot_general, dimension_numbers=dims, preferred_element_type=F32)
    if prec == 1:
        return dot(a.astype(BF16), b.astype(BF16))
    if prec == 3:
        ah, al = _split2(a)
        bh, bl = _split2(b)
        return dot(ah, bh) + (dot(ah, bl) + dot(al, bh))
    if prec == "a":
        bb = b.astype(BF16)
        h, m, l = _split3(a)
        return dot(h, bb) + (dot(m, bb) + dot(l, bb))
    if prec == "b":
        ab = a.astype(BF16)
        h, m, l = _split3(b)
        return dot(ab, h) + (dot(ab, m) + dot(ab, l))
    raise ValueError(prec)


def _sigmoid(x):
    return 1.0 / (1.0 + jnp.exp(-x))


def _softplus(x):
    return jnp.maximum(x, 0.0) + jnp.log1p(jnp.exp(-jnp.abs(x)))


def _iota2(shape, axis):
    return lax.broadcasted_iota(jnp.int32, shape, axis)


def _log2(n):
    assert n & (n - 1) == 0
    return n.bit_length() - 1


def _onehot(mask, dtype=BF16):
    return jnp.where(mask, 1.0, 0.0).astype(dtype)


def _cparams(*sem):
    return pltpu.CompilerParams(dimension_semantics=sem, vmem_limit_bytes=VMEM_LIMIT)


def _dense_kernel(*refs, rms, swiglu, residual, stage):
    it = iter(refs)
    a_ref = next(it)
    g_ref = next(it) if rms else None
    w_ref = next(it)
    w2_ref = next(it) if swiglu else None
    r_ref = next(it) if residual else None
    o_ref = next(it)
    a_sc = next(it) if stage else None

    if stage:
        @pl.when(pl.program_id(1) == 0)
        def _():
            a = a_ref[...].astype(F32)
            if rms:
                a = a * lax.rsqrt(jnp.mean(a * a, axis=-1, keepdims=True) + NORM_EPS) * g_ref[...]
            a_sc[...] = a.astype(BF16)

        a = a_sc[...]
    else:
        a = a_ref[...]
    y = jnp.dot(a, w_ref[...].astype(BF16), preferred_element_type=F32)
    if swiglu:
        u = jnp.dot(a, w2_ref[...].astype(BF16), preferred_element_type=F32)
        y = y * _sigmoid(y) * u
    if residual:
        y = y + r_ref[...]
    o_ref[...] = y.astype(o_ref.dtype)


def _dense(a, w, layer, *, tm, tn, g=None, residual=None, swiglu=False, out_dtype=F32, name):
    m, k = a.shape
    n = w.shape[2] // 2 if swiglu else w.shape[2]
    assert m % tm == 0 and n % tn == 0
    rms = g is not None
    stage = rms or a.dtype != BF16
    nj = n // tn
    in_specs = [pl.BlockSpec((tm, k), lambda i, j: (i, 0))]
    args = [a]
    if rms:
        in_specs.append(pl.BlockSpec((1, k), lambda i, j: (0, 0)))
        args.append(g.reshape(1, k))
    in_specs.append(pl.BlockSpec((None, k, tn), lambda i, j: (layer, 0, j)))
    args.append(w)
    if swiglu:
        in_specs.append(pl.BlockSpec((None, k, tn), lambda i, j: (layer, 0, j + nj)))
        args.append(w)
    if residual is not None:
        in_specs.append(pl.BlockSpec((tm, tn), lambda i, j: (i, j)))
        args.append(residual)
    return pl.pallas_call(
        functools.partial(_dense_kernel, rms=rms, swiglu=swiglu, residual=residual is not None, stage=stage),
        grid=(m // tm, nj),
        in_specs=in_specs,
        out_specs=pl.BlockSpec((tm, tn), lambda i, j: (i, j)),
        out_shape=jax.ShapeDtypeStruct((m, n), out_dtype),
        scratch_shapes=[pltpu.VMEM((tm, k), BF16)] if stage else [],
        compiler_params=_cparams("arbitrary", "arbitrary"),
        name=name,
    )(*args)


def _segsum64(x, ones_blk):
    parts = [
        _mm(x[:, LANES * i : LANES * (i + 1)], ones_blk, prec="a") for i in range(x.shape[1] // LANES)
    ]
    return jnp.concatenate(parts, axis=1)


def _head_ones():
    r = _iota2((LANES, LANES), 0) >> _log2(RW_HD)
    c = _iota2((LANES, LANES), 1) >> _log2(RW_HD)
    return _onehot(r == c)


def _rw_prep_kernel(
    pr_ref, pk_ref, pv_ref, pz_ref, sr_ref, sk_ref, sv_ref, sz_ref, vec_ref, muz_ref, lora_ref,
    r_o, k_o, v_o, kk_o, bb_o, lw_o, bonus_o, g_o,
    prev_r, prev_k, prev_v, prev_z,
):
    tb = pr_ref.shape[0]

    @pl.when(pl.program_id(1) == 0)
    def _():
        prev_r[...] = sr_ref[0]
        prev_k[...] = sk_ref[0]
        prev_v[...] = sv_ref[0]
        prev_z[...] = sz_ref[0]

    def shift(p_ref, prev, mu):
        p = p_ref[...]
        row = _iota2(p.shape, 0)
        p_prev = jnp.where(row == 0, prev[...], pltpu.roll(p, 1, 0))
        prev[...] = p[tb - 1 : tb, :]
        return p + mu * (p_prev - p)

    vec = vec_ref[...]
    w0, a0, k_k, k_a, r_k = vec[0:1], vec[1:2], vec[2:3], vec[3:4], vec[4:5]
    r = shift(pr_ref, prev_r, vec[5:6])
    k = shift(pk_ref, prev_k, vec[6:7])
    v = shift(pv_ref, prev_v, vec[7:8])
    z = shift(pz_ref, prev_z, muz_ref[...])

    lane = _iota2(z.shape, 1)
    in_w = lane < RW_DECAY_LORA
    in_a = (lane >= RW_DECAY_LORA) & (lane < RW_DECAY_LORA + RW_A_LORA)
    lora = lora_ref[...]
    lw_lin = _mm(jnp.where(in_w, jnp.tanh(z), 0.0), lora, prec=3)
    a_lin = _mm(jnp.where(in_a, z, 0.0), lora, prec=3)
    g = _mm(jnp.where(in_w | in_a, 0.0, _sigmoid(z)), lora, prec=3)

    lw = -jnp.exp(-_softplus(-(w0 + lw_lin)) - 0.5)
    a = _sigmoid(a0 + a_lin)
    ones_blk = _head_ones()
    kk = k * k_k
    kk = kk / jnp.maximum(jnp.sqrt(_segsum64(kk * kk, ones_blk)), 1e-12)
    k = k * (1.0 + (a - 1.0) * k_a)
    r_o[...] = r
    k_o[...] = k
    v_o[...] = v
    kk_o[...] = kk
    bb_o[...] = a * kk
    lw_o[...] = lw
    bonus_o[...] = _segsum64(r * k * r_k, ones_blk) * v
    g_o[...] = g


def _rw_prep(proj, col0, col_z, shift0, vec, muz, lora, *, nseq, t, tb, name):
    w = 512
    nt = t // tb
    row = lambda b, i: b * nt + i
    sr, sk, sv, sz = shift0
    in_specs = [
        pl.BlockSpec((tb, w), lambda b, i: (row(b, i), col0)),
        pl.BlockSpec((tb, w), lambda b, i: (row(b, i), col0 + 1)),
        pl.BlockSpec((tb, w), lambda b, i: (row(b, i), col0 + 2)),
        pl.BlockSpec((tb, 256), lambda b, i: (row(b, i), col_z)),
        pl.BlockSpec((1, 1, w), lambda b, i: (b, 0, 0)),
        pl.BlockSpec((1, 1, w), lambda b, i: (b, 0, 0)),
        pl.BlockSpec((1, 1, w), lambda b, i: (b, 0, 0)),
        pl.BlockSpec((1, 1, 256), lambda b, i: (b, 0, 0)),
        pl.BlockSpec((8, w), lambda b, i: (0, 0)),
        pl.BlockSpec((1, 256), lambda b, i: (0, 0)),
        pl.BlockSpec((256, w), lambda b, i: (0, 0)),
    ]
    out = jax.ShapeDtypeStruct((nseq * t, w), F32)
    return pl.pallas_call(
        _rw_prep_kernel,
        grid=(nseq, nt),
        in_specs=in_specs,
        out_specs=[pl.BlockSpec((tb, w), lambda b, i: (row(b, i), 0))] * 8,
        out_shape=[out] * 8,
        scratch_shapes=[pltpu.VMEM((1, w), F32)] * 3 + [pltpu.VMEM((1, 256), F32)],
        compiler_params=_cparams("arbitrary", "arbitrary"),
        name=name,
    )(proj, proj, proj, proj, sr, sk, sv, sz, vec, muz, lora)


RW_PREC_SOLVE = 1
RW_PREC_APPLY = 1


def _rw_scan_kernel(
    r_ref, k_ref, v_ref, kk_ref, bb_ref, lw_ref, bonus_ref, g_ref, ln_ref, s0_ref,
    o_ref, st_ref, a_sc, *, c, valid, nchunks,
):
    ci = pl.program_id(1)

    @pl.when(ci == 0)
    def _():
        a_sc[...] = s0_ref[...]

    hp = LANES // RW_HD
    n = hp * c
    ti = _iota2((n, n), 0)
    si = _iota2((n, n), 1)
    same_head = (ti >> _log2(c)) == (si >> _log2(c))
    incl = same_head & (si <= ti)
    strict = same_head & (si < ti)
    tri = _onehot(_iota2((c, c), 1) <= _iota2((c, c), 0))
    ri = _iota2((LANES, LANES), 0)
    cj = _iota2((LANES, LANES), 1)
    eye = ri == cj
    blockdiag = (ri >> _log2(RW_HD)) == (cj >> _log2(RW_HD))
    ones_blk = _onehot(blockdiag)
    own_lane = (_iota2((n, 1), 0) >> _log2(c)) == (_iota2((1, LANES), 1) >> _log2(RW_HD))
    live = None
    if valid < c:
        live = (_iota2((c, 1), 0) + ci * c) < valid
    mms = functools.partial(_mm, prec=RW_PREC_SOLVE)
    mma = functools.partial(_mm, prec=RW_PREC_APPLY)

    def dup(x):
        return jnp.concatenate([x] * hp, axis=0)

    def stack(x):
        return jnp.where(own_lane, dup(x), 0.0)

    def fold(x):
        out = x[0:c]
        for h in range(1, hp):
            out = out + x[h * c : (h + 1) * c]
        return out

    nsb, npair = r_ref.shape[0], r_ref.shape[2] // LANES
    chains = [(s, p) for s in range(nsb) for p in range(npair)]
    tiles = range(len(chains))

    def load(ref, masked):
        out = []
        for s, p in chains:
            x = ref[s, :, LANES * p : LANES * (p + 1)]
            out.append(jnp.where(live, x, 0.0) if (masked and live is not None) else x)
        return out

    r_l, v_l = load(r_ref, False), load(v_ref, False)
    k_l, kk_l, bb_l, lw_l = load(k_ref, True), load(kk_ref, True), load(bb_ref, True), load(lw_ref, True)
    lwc_l = [_mm(tri, lw, prec="b") for lw in lw_l]
    lwe_l = [lwc[c - 1 : c, :] for lwc in lwc_l]
    a_st_l, r_t_l, v_st_l, k_h_l, b_h_l, gram_l = [], [], [], [], [], []
    for p in tiles:
        lwc, lwe = lwc_l[p], lwe_l[p]
        e_neg = jnp.exp(-lwc)
        e_end = jnp.exp(lwe - lwc)
        a_st = stack(kk_l[p] * jnp.exp(lwc - lw_l[p]))
        r_t = r_l[p] * jnp.exp(lwc)
        a_st_l.append(a_st)
        r_t_l.append(r_t)
        v_st_l.append(stack(v_l[p]))
        k_h_l.append(k_l[p] * e_end)
        b_h_l.append(bb_l[p] * e_end)
        gram_l.append(mms(jnp.concatenate([a_st, stack(r_t)], axis=0),
                          jnp.concatenate([dup(bb_l[p] * e_neg), dup(k_l[p] * e_neg)], axis=0), NT))
    l_l = [jnp.where(strict, g[:n, :n], 0.0) for g in gram_l]
    mv_l = [mma(jnp.where(strict, g[:n, n:], 0.0), v_st) for g, v_st in zip(gram_l, v_st_l)]
    t_l = [_onehot(ti == si, F32) for _ in tiles]
    lev = 0
    while (1 << lev) < c:
        off = ((ti >> (lev + 1)) == (si >> (lev + 1))) & ((ti >> lev) != (si >> lev))
        tl_l = [mms(t, jnp.where(off, l, 0.0)) for t, l in zip(t_l, l_l)]
        t_l = [t - mms(tl, t) for t, tl in zip(t_l, tl_l)]
        lev += 1
    z_l = [mma(t, jnp.concatenate([a_st, mv], axis=1)) for t, a_st, mv in zip(t_l, a_st_l, mv_l)]
    qz_l = [mma(jnp.where(incl, g[n:, :n], 0.0), z) for g, z in zip(gram_l, z_l)]
    pv_l = [mma(jnp.where(incl, g[n:, n:], 0.0), v_st) for g, v_st in zip(gram_l, v_st_l)]
    bz_l = [mma(b_h, jnp.concatenate([fold(z[:, :LANES]), fold(z[:, LANES:])], axis=1), TN)
            for b_h, z in zip(b_h_l, z_l)]
    kv_l = [mma(k_h, v, TN) for k_h, v in zip(k_h_l, v_l)]
    ya_l = []
    for p in tiles:
        ry = r_t_l[p] - fold(qz_l[p][:, :LANES])
        g_mat = jnp.where(eye, jnp.exp(lwe_l[p]), 0.0) - jnp.where(blockdiag, bz_l[p][:, :LANES], 0.0)
        ya_l.append(mma(jnp.concatenate([ry, g_mat], axis=0), a_sc[chains[p]]))
    ys = [ya[:c] + fold(pv - qz[:, LANES:]) for ya, pv, qz in zip(ya_l, pv_l, qz_l)]
    states = [ya[c:] + jnp.where(blockdiag, kv - bz[:, LANES:], 0.0) for ya, kv, bz in zip(ya_l, kv_l, bz_l)]
    a_sc[...] = jnp.stack(states).reshape(a_sc.shape)
    y = jnp.concatenate([jnp.concatenate(ys[s * npair : (s + 1) * npair], axis=1) for s in range(nsb)], axis=0)
    mean = _segsum64(y, ones_blk) * (1.0 / RW_HD)
    d = y - mean
    var = _segsum64(d * d, ones_blk) * (1.0 / RW_HD)
    yn = (d * lax.rsqrt(var + RW_GN_EPS)).reshape(o_ref.shape)
    o_ref[...] = (yn * ln_ref[0:1, :] + ln_ref[1:2, :] + bonus_ref[...]) * g_ref[...]

    @pl.when(ci == nchunks - 1)
    def _():
        st_ref[...] = a_sc[...]


def _rw_scan(prep, ln, s0, *, nseq, t, c, valid, name):
    w = 512
    nchunks = -(-valid // c)
    npair = w // LANES
    nsb = RW_SEQS_PER_STEP if nseq % RW_SEQS_PER_STEP == 0 else 1
    blk = pl.BlockSpec((nsb, c, w), lambda b, i: (b, i, 0))
    st_spec = pl.BlockSpec((nsb, npair, LANES, LANES), lambda b, i: (b, 0, 0, 0))
    o, st = pl.pallas_call(
        functools.partial(_rw_scan_kernel, c=c, valid=valid, nchunks=nchunks),
        grid=(nseq // nsb, nchunks),
        in_specs=[blk] * 8 + [pl.BlockSpec((2, w), lambda b, i: (0, 0)), st_spec],
        out_specs=[blk, st_spec],
        out_shape=[
            jax.ShapeDtypeStruct((nseq, nchunks * c, w), F32),
            jax.ShapeDtypeStruct((nseq, npair, LANES, LANES), F32),
        ],
        scratch_shapes=[pltpu.VMEM((nsb, npair, LANES, LANES), F32)],
        compiler_params=_cparams("arbitrary", "arbitrary"),
        name=name,
    )(*[a.reshape(nseq, t, w) for a in prep], ln, s0)
    return o.reshape(nseq * nchunks * c, w), st


GL_PREC = 1


def _gla_kernel(*refs, tb, valid, nblocks, dk, nsb):
    seq_refs = [refs[5 * s : 5 * s + 5] for s in range(nsb)]
    ab_ref, abias_ref, ng_ref, s0_ref, o_ref, st_ref, s_sc = refs[5 * nsb :]
    bi = pl.program_id(1)

    @pl.when(bi == 0)
    def _():
        s_sc[...] = s0_ref[...]

    mm = functools.partial(_mm, prec=GL_PREC)
    ti = _iota2((tb, tb), 0)
    si = _iota2((tb, tb), 1)
    tri = _onehot(si <= ti)
    trow = _iota2((tb, 1), 0)
    lane_head = _iota2((1, LANES), 1) >> _log2(dk)
    heads_per_tile = LANES // dk
    own_lane = (_iota2((heads_per_tile * tb, 1), 0) >> _log2(tb)) == lane_head

    def dup(x):
        return jnp.concatenate([x] * heads_per_tile, axis=0)

    def stack(x):
        return jnp.where(own_lane, dup(x), 0.0)

    npair = seq_refs[0][0].shape[1] // LANES
    chains = [(s, p) for s in range(nsb) for p in range(npair)]
    q_l, k_l, la_l = [], [], []
    for s in range(nsb):
        q_ref, k_ref, _, xa_ref, _ = seq_refs[s]
        la_all = -_softplus(-(_mm(xa_ref[...], ab_ref[...], prec=3) + abias_ref[...])) * (1.0 / GL_TAU)
        k_all = k_ref[...]
        if valid < tb:
            live = (trow + bi * tb) < valid
            la_all = jnp.where(live, la_all, 0.0)
            k_all = jnp.where(live, k_all, 0.0)
        for p in range(npair):
            sl = slice(LANES * p, LANES * (p + 1))
            q_l.append(q_ref[:, sl] * (dk**-0.5))
            k_l.append(k_all[:, sl])
            la_l.append(la_all[:, sl])
    b_l = [_mm(tri, la, prec="b") for la in la_l]
    att_l = [jnp.where(dup(ti == si), mm(stack(q), k, NT), 0.0) for q, k in zip(q_l, k_l)]
    lev = 1
    while (1 << lev) <= tb:
        half = 1 << (lev - 1)
        sel = _onehot(si == ((ti >> lev) << lev) + (half - 1))
        upper = ((trow >> (lev - 1)) & 1) == 1
        same = dup((ti >> lev) == (si >> lev))
        bref_l = [_mm(sel, b, prec="b") for b in b_l]
        qs_l = [stack(jnp.where(upper, q * jnp.exp(jnp.minimum(b - br, 0.0)), 0.0)) for q, b, br in zip(q_l, b_l, bref_l)]
        ks_l = [jnp.where(upper, 0.0, k * jnp.exp(jnp.minimum(br - b, 0.0))) for k, b, br in zip(k_l, b_l, bref_l)]
        att_l = [att + jnp.where(same, mm(qs, ks, NT), 0.0) for att, qs, ks in zip(att_l, qs_l, ks_l)]
        lev += 1

    s_old_l = [s_sc[s, p] for s, p in chains]
    inter_l = [mm(stack(q * jnp.exp(b)), s_old, NT) for q, b, s_old in zip(q_l, b_l, s_old_l)]
    outs, states = [], []
    for i, (s, p) in enumerate(chains):
        b_end = b_l[i][tb - 1 : tb, :]
        ke = k_l[i] * jnp.exp(b_end - b_l[i])
        s_new = s_old_l[i] * jnp.exp(b_end)
        for h in range(heads_per_tile):
            hv = heads_per_tile * p + h
            v = seq_refs[s][2][:, LANES * hv : LANES * (hv + 1)]
            o = mm(att_l[i][h * tb : (h + 1) * tb], v) + inter_l[i][h * tb : (h + 1) * tb]
            s_new = s_new + mm(v, jnp.where(lane_head == h, ke, 0.0), TN)
            outs.append(o * lax.rsqrt(jnp.mean(o * o, axis=-1, keepdims=True) + NORM_EPS) * ng_ref[...])
        states.append(s_new)

    s_sc[...] = jnp.stack(states).reshape(s_sc.shape)
    per_seq = len(outs) // nsb
    for s in range(nsb):
        gg = seq_refs[s][4][...]
        o_ref[s] = jnp.concatenate(outs[s * per_seq : (s + 1) * per_seq], axis=1) * (gg * _sigmoid(gg))

    @pl.when(bi == nblocks - 1)
    def _():
        st_ref[...] = s_sc[...]


def _gla(proj, cols, ab, abias, ng, s0, *, nseq, t, tb, valid, name):
    cq, ck, cv, cxa, cgg = cols
    nb = t // tb
    nrun = -(-valid // tb)
    npair = 256 // LANES
    nsb = GL_SEQS_PER_STEP if nseq % GL_SEQS_PER_STEP == 0 else 1
    st_spec = pl.BlockSpec((nsb, npair, LANES, LANES), lambda b, i: (b, 0, 0, 0))

    def seq_specs(s):
        row = lambda b, i: (b * nsb + s) * nb + i
        return [
            pl.BlockSpec((tb, 256), lambda b, i: (row(b, i), cq)),
            pl.BlockSpec((tb, 256), lambda b, i: (row(b, i), ck)),
            pl.BlockSpec((tb, 512), lambda b, i: (row(b, i), cv)),
            pl.BlockSpec((tb, LANES), lambda b, i: (row(b, i), cxa)),
            pl.BlockSpec((tb, 512), lambda b, i: (row(b, i), cgg)),
        ]

    o, st = pl.pallas_call(
        functools.partial(_gla_kernel, tb=tb, valid=valid, nblocks=nrun, dk=256 // GL_HEADS, nsb=nsb),
        grid=(nseq // nsb, nrun),
        in_specs=[spec for s in range(nsb) for spec in seq_specs(s)]
        + [
            pl.BlockSpec((LANES, 256), lambda b, i: (0, 0)),
            pl.BlockSpec((1, 256), lambda b, i: (0, 0)),
            pl.BlockSpec((1, LANES), lambda b, i: (0, 0)),
            st_spec,
        ],
        out_specs=[pl.BlockSpec((nsb, tb, 512), lambda b, i: (b, i, 0)), st_spec],
        out_shape=[
            jax.ShapeDtypeStruct((nseq, nrun * tb, 512), F32),
            jax.ShapeDtypeStruct((nseq, npair, LANES, LANES), F32),
        ],
        scratch_shapes=[pltpu.VMEM((nsb, npair, LANES, LANES), F32)],
        compiler_params=_cparams("arbitrary", "arbitrary"),
        name=name,
    )(*([proj] * (5 * nsb)), ab, abias, ng, s0)
    return o.reshape(nseq * nrun * tb, 512), st


def _fx_prep_kernel(*refs, stacked):
    q_ref, k_ref, v_ref, fl_ref, qg_ref, kg_ref, bf_ref = refs[:7]
    qb_o, kf_o, vf_o, kb_o, vb_o, lf_o, c_o, carry = refs[9:] if stacked else refs[7:]
    tb = q_ref.shape[0]

    @pl.when(pl.program_id(1) == 0)
    def _():
        carry[...] = jnp.zeros_like(carry)

    def headnorm(x, gain):
        return x * lax.rsqrt(jnp.mean(x * x, axis=-1, keepdims=True) + NORM_EPS) * gain

    nh = q_ref.shape[1] // FX_HD
    for h in range(nh):
        sl = slice(FX_HD * h, FX_HD * (h + 1))
        qb_o[:, sl] = (headnorm(q_ref[:, sl], qg_ref[...]) * (FX_HD**-0.5 * LOG2E)).astype(BF16)
        kn = headnorm(k_ref[:, sl], kg_ref[...])
        kb_o[:, sl] = kn.astype(BF16)
        kf_o[pl.ds(h, tb, stride=nh), :] = kn
        vf_o[pl.ds(h, tb, stride=nh), :] = v_ref[:, sl]
    vb_o[...] = v_ref[...].astype(BF16)
    logf = -_softplus(-(fl_ref[...] + bf_ref[...]))
    lf_o[...] = logf
    tri = _onehot(_iota2((tb, tb), 1) <= _iota2((tb, tb), 0))
    c = _mm(tri, logf, prec="b") + carry[...]
    c_o[...] = c
    carry[...] = c[tb - 1 : tb, :]


def _fx_prep(proj, cols, qg, kg, bf, *, nseq, t, tb, name, stack=None):
    cq, ck, cv, cfl = cols
    nt = t // tb
    row = lambda b, i: b * nt + i
    w = 1024
    nh = w // FX_HD
    wide = pl.BlockSpec((tb, w), lambda b, i: (row(b, i), 0))
    thin = pl.BlockSpec((tb, LANES), lambda b, i: (row(b, i), 0))
    vecspec = pl.BlockSpec((1, LANES), lambda b, i: (0, 0))
    rows = nseq * t
    extra_in, extra_args, aliases = [], [], {}
    kv_spec = pl.BlockSpec((tb * nh, FX_HD), lambda b, i: (row(b, i), 0))
    kv_shape = jax.ShapeDtypeStruct((rows * nh, FX_HD), F32)
    if stack is not None:
        k_all, v_all, layer = stack
        extra_in = [pl.BlockSpec(memory_space=pl.ANY)] * 2
        extra_args = [k_all, v_all]
        aliases = {7: 1, 8: 2}
        kv_spec = pl.BlockSpec((None, None, tb * nh, FX_HD), lambda b, i: (b, layer, i, 0))
        kv_shape = jax.ShapeDtypeStruct(k_all.shape, F32)
    return pl.pallas_call(
        functools.partial(_fx_prep_kernel, stacked=stack is not None),
        grid=(nseq, nt),
        in_specs=[
            pl.BlockSpec((tb, w), lambda b, i: (row(b, i), cq)),
            pl.BlockSpec((tb, w), lambda b, i: (row(b, i), ck)),
            pl.BlockSpec((tb, w), lambda b, i: (row(b, i), cv)),
            pl.BlockSpec((tb, LANES), lambda b, i: (row(b, i), cfl)),
            vecspec, vecspec, vecspec,
        ] + extra_in,
        out_specs=[wide, kv_spec, kv_spec, wide, wide, thin, thin],
        out_shape=[
            jax.ShapeDtypeStruct((rows, w), BF16),
            kv_shape,
            kv_shape,
            jax.ShapeDtypeStruct((rows, w), BF16),
            jax.ShapeDtypeStruct((rows, w), BF16),
            jax.ShapeDtypeStruct((rows, LANES), F32),
            jax.ShapeDtypeStruct((rows, LANES), F32),
        ],
        scratch_shapes=[pltpu.VMEM((1, LANES), F32)],
        input_output_aliases=aliases,
        compiler_params=_cparams("arbitrary", "arbitrary"),
        name=name,
    )(proj, proj, proj, proj, qg, kg, bf, *extra_args)


def _with_ones(v):
    return jnp.concatenate([v, jnp.ones_like(v)], axis=1)


def _flash_kernel(q_ref, k_ref, v_ref, c_ref, cref_ref, o_ref, m_sc, acc_sc, *, tq, tk, hb):
    qi = pl.program_id(2)
    kj = pl.program_id(3)

    @pl.when(kj == 0)
    def _():
        m_sc[...] = jnp.full_like(m_sc, -jnp.inf)
        acc_sc[...] = jnp.zeros_like(acc_sc)

    def update(masked):
        heads = range(hb)
        sl = [slice(FX_HD * h, FX_HD * (h + 1)) for h in heads]
        s_l = [lax.dot_general(q_ref[:, sl[h]], k_ref[:, sl[h]], NT, preferred_element_type=F32) for h in heads]
        s_l = [s_l[h] + (cref_ref[0, h, :, 0:1] - c_ref[0, h]) * LOG2E for h in heads]
        if masked:
            keep = (kj * tk + _iota2((tq, tk), 1)) <= (qi * tq + _iota2((tq, tk), 0))
            s_l = [jnp.where(keep, s, -jnp.inf) for s in s_l]
        m_old = [m_sc[h] for h in heads]
        m_new = [jnp.maximum(m_old[h], jnp.max(s_l[h], axis=-1, keepdims=True)) for h in heads]
        p_l = [jnp.exp2(s_l[h] - pltpu.repeat(m_new[h], tk // LANES, axis=1)) for h in heads]
        pv_l = [jnp.dot(p_l[h].astype(BF16), _with_ones(v_ref[:, sl[h]]), preferred_element_type=F32) for h in heads]
        for h in heads:
            alpha = jnp.exp2(m_old[h] - m_new[h])
            acc_sc[h] = pltpu.repeat(alpha, 2, axis=1) * acc_sc[h] + pv_l[h]
            m_sc[h] = m_new[h]

    first_key, last_key = kj * tk, kj * tk + (tk - 1)
    first_query, last_query = qi * tq, qi * tq + (tq - 1)

    @pl.when(last_key <= first_query)
    def _():
        update(False)

    @pl.when((last_key > first_query) & (first_key <= last_query))
    def _():
        update(True)

    @pl.when(kj == pl.num_programs(3) - 1)
    def _():
        for h in range(hb):
            acc = acc_sc[h]
            o_ref[:, FX_HD * h : FX_HD * (h + 1)] = acc[:, :FX_HD] / acc[:, FX_HD:]


def _flash(qb, kb, vb, c_row, *, nseq, t, tq, tk, name):
    nh = qb.shape[1] // FX_HD
    hb = FLASH_HEADS_PER_STEP if nh % FLASH_HEADS_PER_STEP == 0 else 1
    nq, nk = t // tq, t // tk
    last = lambda i: (i * tq + tq - 1) // tk
    w = hb * FX_HD
    return pl.pallas_call(
        functools.partial(_flash_kernel, tq=tq, tk=tk, hb=hb),
        grid=(nseq, nh // hb, nq, nk),
        in_specs=[
            pl.BlockSpec((tq, w), lambda b, h, i, j: (b * nq + i, h)),
            pl.BlockSpec((tk, w), lambda b, h, i, j: (b * nk + jnp.minimum(j, last(i)), h)),
            pl.BlockSpec((tk, w), lambda b, h, i, j: (b * nk + jnp.minimum(j, last(i)), h)),
            pl.BlockSpec((1, hb, 1, tk), lambda b, h, i, j: (b, h, 0, jnp.minimum(j, last(i)))),
            pl.BlockSpec((1, hb, 1, LANES), lambda b, h, i, j: (b, h, 0, i * (tq // LANES))),
        ],
        out_specs=pl.BlockSpec((tq, w), lambda b, h, i, j: (b * nq + i, h)),
        out_shape=jax.ShapeDtypeStruct((nseq * t, nh * FX_HD), F32),
        scratch_shapes=[pltpu.VMEM((hb, tq, LANES), F32), pltpu.VMEM((hb, tq, 2 * FX_HD), F32)],
        compiler_params=_cparams("arbitrary", "arbitrary", "arbitrary", "arbitrary"),
        name=name,
    )(qb, kb, vb, c_row, c_row)


def _decode_kernel(pt_ref, q_ref, kn_ref, vn_ref, cn_ref, *rest, nq, nh, pp):
    kc_refs, vc_refs, lf_refs = rest[0:pp], rest[pp : 2 * pp], rest[2 * pp : 3 * pp]
    o_ref, m_sc, acc_sc, carry = rest[3 * pp :]
    j = pl.program_id(1)
    heads = range(nh)
    q_l = [q_ref[:, FX_HD * h : FX_HD * (h + 1)] for h in heads]

    def attend(pages, causal):
        kv = [[load(h) for h in heads] for load, _ in pages]
        s = [[lax.dot_general(q_l[h], kv[i][h][0], NT, preferred_element_type=F32) + pages[i][1][h : h + 1, :]
              for h in heads] for i in range(len(pages))]
        if causal:
            keep = _iota2((nq, PAGE), 1) <= _iota2((nq, PAGE), 0)
            s = [[jnp.where(keep, x, -jnp.inf) for x in row] for row in s]
        for h in heads:
            m_old = m_sc[h]
            m_new = m_old
            for i in range(len(pages)):
                m_new = jnp.maximum(m_new, jnp.max(s[i][h], axis=-1, keepdims=True))
            acc = pltpu.repeat(jnp.exp2(m_old - m_new), 2, axis=1) * acc_sc[h]
            for i in range(len(pages)):
                p = jnp.exp2(s[i][h] - m_new)
                acc = acc + jnp.dot(p.astype(BF16), _with_ones(kv[i][h][1]), preferred_element_type=F32)
            acc_sc[h] = acc
            m_sc[h] = m_new

    @pl.when(j == 0)
    def _():
        m_sc[...] = jnp.full_like(m_sc, -jnp.inf)
        acc_sc[...] = jnp.zeros_like(acc_sc)
        carry[...] = jnp.zeros_like(carry)

        def new_rows(h):
            sl = slice(FX_HD * h, FX_HD * (h + 1))
            return kn_ref[:, sl], vn_ref[:, sl]

        attend([(new_rows, -cn_ref[0] * LOG2E)], True)

    @pl.when(j > 0)
    def _():
        after = _onehot(_iota2((PAGE, PAGE), 0) > _iota2((PAGE, PAGE), 1))
        both = jnp.concatenate([after, jnp.ones((PAGE, PAGE), BF16)], axis=1)
        run = carry[...]
        pages = []
        for i in range(pp):

            def cached_rows(h, i=i):
                rows = pl.ds(h, PAGE, stride=nh)
                return kc_refs[i][rows, :].astype(BF16), vc_refs[i][rows, :].astype(BF16)

            sums = _mm(lf_refs[i][...], both, prec="a")
            pages.append((cached_rows, (run + sums[:, :PAGE]) * LOG2E))
            run = run + sums[:, PAGE:]
        carry[...] = run
        attend(pages, False)

    @pl.when(j == pl.num_programs(1) - 1)
    def _():
        for h in heads:
            acc = acc_sc[h]
            o_ref[:, FX_HD * h : FX_HD * (h + 1)] = acc[:, :FX_HD] / acc[:, FX_HD:]


def _decode(qb, kb, vb, cn_t, cache_k, cache_v, cache_logf, page_table, layer, *, nq, name):
    nseq, npages = page_table.shape
    pool, depth, _, nh, hd = cache_k.shape
    w = nh * hd
    pp = DEC_PAGES_PER_STEP if npages % DEC_PAGES_PER_STEP == 0 else 1
    ck = cache_k.reshape(pool, depth, PAGE * nh, hd)
    cv = cache_v.reshape(pool, depth, PAGE * nh, hd)
    lf_t = jnp.swapaxes(cache_logf, 2, 3)

    def page_of(i):
        def index(b, j, pt):
            back = (jnp.maximum(j, 1) - 1) * pp + i
            return (pt[b * npages + (npages - 1 - back)], layer, 0, 0)

        return index

    grid_spec = pltpu.PrefetchScalarGridSpec(
        num_scalar_prefetch=1,
        grid=(nseq, 1 + npages // pp),
        in_specs=[
            pl.BlockSpec((nq, w), lambda b, j, pt: (b * (PAGE // nq), 0)),
            pl.BlockSpec((PAGE, w), lambda b, j, pt: (b, 0)),
            pl.BlockSpec((PAGE, w), lambda b, j, pt: (b, 0)),
            pl.BlockSpec((1, nh, PAGE), lambda b, j, pt: (b, 0, 0)),
        ]
        + [pl.BlockSpec((None, None, PAGE * nh, hd), page_of(i)) for i in range(pp)] * 2
        + [pl.BlockSpec((None, None, nh, PAGE), page_of(i)) for i in range(pp)],
        out_specs=pl.BlockSpec((nq, w), lambda b, j, pt: (b, 0)),
        scratch_shapes=[
            pltpu.VMEM((nh, nq, LANES), F32),
            pltpu.VMEM((nh, nq, 2 * hd), F32),
            pltpu.VMEM((nh, PAGE), F32),
        ],
    )
    return pl.pallas_call(
        functools.partial(_decode_kernel, nq=nq, nh=nh, pp=pp),
        grid_spec=grid_spec,
        out_shape=jax.ShapeDtypeStruct((nseq * nq, w), F32),
        compiler_params=_cparams("arbitrary", "arbitrary"),
        name=name,
    )(page_table.reshape(-1), qb, kb, vb, cn_t, *([ck] * pp), *([cv] * pp), *([lf_t] * pp))


RW_CHUNK = 64
RW_SEQS_PER_STEP = 2
GL_BLOCK = 128
GL_SEQS_PER_STEP = 2
FLASH_HEADS_PER_STEP = 2
DEC_Q_ROWS = 16
DEC_PAGES_PER_STEP = 8
TM = 768
COL_FX_Q, COL_FX_K, COL_FX_V = 0, 1024, 2048
COL_RW_R = 3072
COL_GL_V, COL_GL_GG = 4608, 5120
COL_GL_Q, COL_GL_K = 5632, 5888
COL_RW_Z = 6144
COL_FX_FL = 6400
COL_GL_XA = 6528
COLS_PAD = 6656


def _pad_cols(x, width):
    return jnp.pad(x, [(0, 0)] * (x.ndim - 1) + [(0, width - x.shape[-1])])


def _relayout_in(w, rw_w, fx_w, gl_kw, gl_vw, nh_fx):
    rw_z = RW_DECAY_LORA + RW_A_LORA + RW_G_LORA
    o = 0
    rw_rkv = w[..., o : o + 3 * rw_w]; o += 3 * rw_w
    rw_zz = w[..., o : o + rw_z]; o += rw_z
    fx_qkv = w[..., o : o + 3 * fx_w]; o += 3 * fx_w
    fx_fl = w[..., o : o + nh_fx]; o += nh_fx
    gl_qk = w[..., o : o + 2 * gl_kw]; o += 2 * gl_kw
    gl_v = w[..., o : o + gl_vw]; o += gl_vw
    gl_xa = w[..., o : o + GL_GATE_LORA]; o += GL_GATE_LORA
    gl_gg = w[..., o : o + gl_vw]; o += gl_vw
    assert o == w.shape[-1]
    return jnp.concatenate(
        [fx_qkv, rw_rkv, gl_v, gl_gg, gl_qk, _pad_cols(rw_zz, 256), _pad_cols(fx_fl, LANES), _pad_cols(gl_xa, LANES)],
        axis=-1,
    )


def _mixers(proj, layer, lp, states, paged, *, nseq, t, valid, tag, kv_stack=None):
    (shift0, rw_s0, gl_s0) = states
    tb = min(t, 256)
    prep = _rw_prep(proj, COL_RW_R // 512, COL_RW_Z // 256, shift0, lp["rw_vec"], lp["rw_muz"], lp["rw_lora"],
                    nseq=nseq, t=t, tb=tb, name=f"rw_prep_{tag}")
    o_rw, rw_state = _rw_scan(prep, lp["rw_ln"], rw_s0, nseq=nseq, t=t, c=min(t, RW_CHUNK), valid=valid,
                              name=f"rw_scan_{tag}")
    o_gl, gl_state = _gla(
        proj, (COL_GL_Q // 256, COL_GL_K // 256, COL_GL_V // 512, COL_GL_XA // LANES, COL_GL_GG // 512),
        lp["gl_ab"], lp["gl_abias"], lp["gl_ng"], gl_s0, nseq=nseq, t=t, tb=min(t, GL_BLOCK), valid=valid,
        name=f"gla_{tag}")
    qb, kf, vf, kb, vb, logf, c = _fx_prep(
        proj, (COL_FX_Q // 1024, COL_FX_K // 1024, COL_FX_V // 1024, COL_FX_FL // LANES),
        lp["fx_qg"], lp["fx_kg"], lp["fx_bf"], nseq=nseq, t=t, tb=tb, name=f"fx_prep_{tag}",
        stack=None if kv_stack is None else (*kv_stack, layer))
    nh = kb.shape[1] // FX_HD
    c_t = jnp.transpose(c.reshape(nseq, t, LANES)[:, :, :nh], (0, 2, 1))
    if paged is None:
        blk = min(t, 512)
        o_fx = _flash(qb, kb, vb, c_t[:, :, None, :], nseq=nseq, t=t, tq=blk, tk=blk, name=f"flash_{tag}")
    else:
        cache_k, cache_v, cache_logf, page_table = paged
        nq = DEC_Q_ROWS
        assert t == PAGE and valid <= nq
        o_fx = _decode(qb, kb, vb, c_t, cache_k, cache_v, cache_logf, page_table, layer, nq=nq, name=f"decode_{tag}")
        o_fx = o_fx.reshape(nseq, nq, -1)[:, :valid].reshape(nseq * valid, -1)
    return o_rw, o_fx, o_gl, kf, logf, rw_state, gl_state, vf


def kernel(x_prompt, x_sample, cache_k, cache_v, cache_logf, state_rwkv, state_rwkv_shift, state_gla,
           page_table, attn_norm_g, w_in, rw_mu, rw_w_b, rw_w0, rw_a_b, rw_a0, rw_g_b, rw_k_k, rw_k_a,
           rw_r_k, rw_lnx_g, rw_lnx_b, fx_qn_g, fx_kn_g, fx_b_f, gl_alpha_b, gl_alpha_bias, gl_norm_g,
           w_out, ffn_norm_g, w_gu, w_down):
    nb, seq, d_model = x_prompt.shape
    ns, dseq, _ = x_sample.shape
    depth = w_in.shape[0]
    rw_w = rw_w0.shape[1]
    rw_heads = rw_w // RW_HD
    nh_fx = fx_b_f.shape[1]
    fx_w = nh_fx * FX_HD
    gl_kw = gl_alpha_b.shape[2]
    gl_dk = gl_kw // GL_HEADS
    gl_dv = gl_norm_g.shape[1]
    gl_vw = GL_HEADS * gl_dv
    rw_cols = 3 * rw_w + RW_DECAY_LORA + RW_A_LORA + RW_G_LORA
    assert (rw_w, fx_w, gl_kw, gl_vw, FX_HD, gl_dv) == (512, 1024, 256, 512, 128, 128)
    n_prompt = nb * seq
    n_sample = ns * dseq
    rows = n_prompt + n_sample
    tm = TM if rows >= TM else 64
    rows_pad = -(-rows // tm) * tm
    relayout = functools.partial(_relayout_in, rw_w=rw_w, fx_w=fx_w, gl_kw=gl_kw, gl_vw=gl_vw, nh_fx=nh_fx)

    w_in_p = relayout(w_in).astype(BF16)
    w_out_b, w_gu_b, w_down_b = w_out, w_gu, w_down
    rw_mu_r, rw_mu_k, rw_mu_v, rw_mu_z = (
        rw_mu[:, :rw_w], rw_mu[:, rw_w : 2 * rw_w], rw_mu[:, 2 * rw_w : 3 * rw_w], rw_mu[:, 3 * rw_w :])
    layers = []
    for l in range(depth):
        layers.append(dict(
            rw_vec=jnp.stack([rw_w0[l], rw_a0[l], rw_k_k[l], rw_k_a[l], rw_r_k[l].reshape(-1),
                              rw_mu_r[l], rw_mu_k[l], rw_mu_v[l]]),
            rw_muz=_pad_cols(rw_mu_z[l][None], 256),
            rw_lora=jnp.pad(jnp.concatenate([rw_w_b[l], rw_a_b[l], rw_g_b[l]], axis=0),
                            ((0, 256 - (rw_cols - 3 * rw_w)), (0, 0))),
            rw_ln=jnp.stack([rw_lnx_g[l], rw_lnx_b[l]]),
            fx_qg=fx_qn_g[l][None], fx_kg=fx_kn_g[l][None], fx_bf=_pad_cols(fx_b_f[l][None], LANES),
            gl_ab=jnp.pad(gl_alpha_b[l], ((0, LANES - GL_GATE_LORA), (0, 0))),
            gl_abias=gl_alpha_bias[l][None], gl_ng=gl_norm_g[l][None],
        ))

    def rw_state_in(s):
        n = s.shape[0]
        a = jnp.swapaxes(s, -1, -2).reshape(n, rw_heads // 2, 2, RW_HD, RW_HD)
        z = jnp.zeros_like(a[:, :, 0])
        top = jnp.concatenate([a[:, :, 0], z], axis=-1)
        bot = jnp.concatenate([z, a[:, :, 1]], axis=-1)
        return jnp.concatenate([top, bot], axis=-2)

    def rw_state_out(a):
        n = a.shape[0]
        h0 = a[:, :, :RW_HD, :RW_HD]
        h1 = a[:, :, RW_HD:, RW_HD:]
        return jnp.swapaxes(jnp.stack([h0, h1], axis=2).reshape(n, rw_heads, RW_HD, RW_HD), -1, -2)

    def gl_state_in(s):
        n = s.shape[0]
        a = jnp.swapaxes(s, -1, -2).reshape(n, GL_HEADS // 2, 2, gl_dv, gl_dk)
        return jnp.transpose(a, (0, 1, 3, 2, 4)).reshape(n, GL_HEADS // 2, gl_dv, 2 * gl_dk)

    def gl_state_out(a):
        n = a.shape[0]
        a = jnp.transpose(a.reshape(n, GL_HEADS // 2, gl_dv, 2, gl_dk), (0, 1, 3, 2, 4))
        return jnp.swapaxes(a.reshape(n, GL_HEADS, gl_dv, gl_dk), -1, -2)

    def shift_in(s):
        return (s[:, None, :rw_w], s[:, None, rw_w : 2 * rw_w], s[:, None, 2 * rw_w : 3 * rw_w],
                _pad_cols(s[:, None, 3 * rw_w :], 256))

    def shift_out(proj_rows):
        return jnp.concatenate([proj_rows[:, COL_RW_R : COL_RW_R + 3 * rw_w],
                                proj_rows[:, COL_RW_Z : COL_RW_Z + rw_cols - 3 * rw_w]], axis=-1)

    x = jnp.concatenate([x_prompt.reshape(n_prompt, d_model), x_sample.reshape(n_sample, d_model),
                         jnp.zeros((rows_pad - rows, d_model), x_prompt.dtype)], axis=0)
    t_s = PAGE
    p_states = (shift_in(jnp.zeros((nb, rw_cols), F32)), jnp.zeros((nb, rw_heads // 2, LANES, LANES), F32),
                jnp.zeros((nb, GL_HEADS // 2, LANES, LANES), F32))
    outs = {k: [] for k in ("plf", "prw", "psh", "pgl", "sk", "sv", "slf", "srw", "ssh", "sgl")}
    pk_all = jnp.zeros((nb, depth, seq * nh_fx, FX_HD), F32)
    pv_all = jnp.zeros((nb, depth, seq * nh_fx, FX_HD), F32)
    for l in range(depth):
        lp = layers[l]
        proj = _dense(x, w_in_p, l, tm=tm, tn=512, g=attn_norm_g[l], name=f"w_in_{l}")
        proj_s = jnp.pad(proj[n_prompt:rows].reshape(ns, dseq, COLS_PAD), ((0, 0), (0, t_s - dseq), (0, 0)))
        proj_s = proj_s.reshape(ns * t_s, COLS_PAD)
        s_states = (shift_in(state_rwkv_shift[:, l]), rw_state_in(state_rwkv[:, l]), gl_state_in(state_gla[:, l]))
        po = _mixers(proj, l, lp, p_states, None, nseq=nb, t=seq, valid=seq, tag=f"p{l}", kv_stack=(pk_all, pv_all))
        pk_all, pv_all = po[3], po[7]
        so = _mixers(proj_s, l, lp, s_states, (cache_k, cache_v, cache_logf, page_table),
                     nseq=ns, t=t_s, valid=dseq, tag=f"s{l}")

        def unpad(a):
            return a.reshape(ns, a.shape[0] // ns, -1)[:, :dseq].reshape(n_sample, -1)

        mix = jnp.concatenate([
            jnp.concatenate([po[0], po[1], po[2]], axis=1),
            jnp.concatenate([unpad(so[0]), so[1], unpad(so[2])], axis=1),
            jnp.zeros((rows_pad - rows, d_model), F32)], axis=0)
        x = _dense(mix, w_out_b, l, tm=tm, tn=512, residual=x, name=f"w_out_{l}")
        act = _dense(x, w_gu_b, l, tm=tm, tn=512, g=ffn_norm_g[l], swiglu=True, out_dtype=BF16, name=f"w_gu_{l}")
        x = _dense(act, w_down_b, l, tm=tm, tn=256, residual=x, name=f"w_down_{l}")

        outs["plf"].append(po[4][:, :nh_fx].reshape(nb, seq, nh_fx))
        outs["prw"].append(rw_state_out(po[5]))
        outs["psh"].append(shift_out(jnp.stack([proj[b * seq + seq - 1] for b in range(nb)])))
        outs["pgl"].append(gl_state_out(po[6]))
        outs["sk"].append(so[3].reshape(ns, t_s, nh_fx, FX_HD)[:, :dseq])
        outs["sv"].append(so[7].reshape(ns, t_s, nh_fx, FX_HD)[:, :dseq])
        outs["slf"].append(unpad(so[4])[:, :nh_fx].reshape(ns, dseq, nh_fx))
        outs["srw"].append(rw_state_out(so[5]))
        outs["ssh"].append(shift_out(proj[n_prompt:rows].reshape(ns, dseq, COLS_PAD)[:, dseq - 1]))
        outs["sgl"].append(gl_state_out(so[6]))

    st = {k: jnp.stack(v, axis=1) for k, v in outs.items()}
    y_prompt = x[:n_prompt].reshape(nb, seq, d_model)
    y_sample = x[n_prompt:rows].reshape(ns, dseq, d_model)
    kv_shape = (nb, depth, seq, nh_fx, FX_HD)
    return (y_prompt, y_sample, pk_all.reshape(kv_shape), pv_all.reshape(kv_shape), st["plf"], st["prw"], st["psh"], st["pgl"],
            st["sk"], st["sv"], st["slf"], st["srw"], st["ssh"], st["sgl"])
```

```python
import functools

import jax
import jax.numpy as jnp
from jax import lax
from jax.experimental import pallas as pl
from jax.experimental.pallas import tpu as pltpu

F32 = jnp.float32
BF16 = jnp.bfloat16

RW_HD = 64
RW_DECAY_LORA = 32
RW_A_LORA = 32
RW_G_LORA = 96
RW_GN_EPS = 64e-5
FX_HD = 128
GL_HEADS = 4
GL_GATE_LORA = 16
GL_TAU = 16.0
NORM_EPS = 1e-6
PAGE = 128
LOG2E = 1.4426950408889634

LANES = 128
VMEM_LIMIT = 56 * 1024 * 1024

NN = (((1,), (0,)), ((), ()))
NT = (((1,), (1,)), ((), ()))
TN = (((0,), (0,)), ((), ()))


def _split2(x):
    hi = x.astype(BF16)
    lo = (x - hi.astype(F32)).astype(BF16)
    return hi, lo


def _split3(x):
    hi = x.astype(BF16)
    r = x - hi.astype(F32)
    mid = r.astype(BF16)
    lo = (r - mid.astype(F32)).astype(BF16)
    return hi, mid, lo


def _mm(a, b, dims=NN, prec=1):
    dot = functools.partial(lax.dot_general, dimension_numbers=dims, preferred_element_type=F32)
    if prec == 1:
        return dot(a.astype(BF16), b.astype(BF16))
    if prec == 3:
        ah, al = _split2(a)
        bh, bl = _split2(b)
        return dot(ah, bh) + (dot(ah, bl) + dot(al, bh))
    if prec == "a":
        bb = b.astype(BF16)
        h, m, l = _split3(a)
        return dot(h, bb) + (dot(m, bb) + dot(l, bb))
    if prec == "b":
        ab = a.astype(BF16)
        h, m, l = _split3(b)
        return dot(ab, h) + (dot(ab, m) + dot(ab, l))
    raise ValueError(prec)


def _sigmoid(x):
    return 1.0 / (1.0 + jnp.exp(-x))


def _softplus(x):
    return jnp.maximum(x, 0.0) + jnp.log1p(jnp.exp(-jnp.abs(x)))


def _iota2(shape, axis):
    return lax.broadcasted_iota(jnp.int32, shape, axis)


def _log2(n):
    assert n & (n - 1) == 0
    return n.bit_length() - 1


def _onehot(mask, dtype=BF16):
    return jnp.where(mask, 1.0, 0.0).astype(dtype)


def _cparams(*sem):
    return pltpu.CompilerParams(dimension_semantics=sem, vmem_limit_bytes=VMEM_LIMIT)


def _dense_kernel(*refs, rms, swiglu, residual, stage):
    it = iter(refs)
    a_ref = next(it)
    g_ref = next(it) if rms else None
    w_ref = next(it)
    w2_ref = next(it) if swiglu else None
    r_ref = next(it) if residual else None
    o_ref = next(it)
    a_sc = next(it) if stage else None

    if stage:
        @pl.when(pl.program_id(1) == 0)
        def _():
            a = a_ref[...].astype(F32)
            if rms:
                a = a * lax.rsqrt(jnp.mean(a * a, axis=-1, keepdims=True) + NORM_EPS) * g_ref[...]
            a_sc[...] = a.astype(BF16)

        a = a_sc[...]
    else:
        a = a_ref[...]
    y = jnp.dot(a, w_ref[...], preferred_element_type=F32)
    if swiglu:
        u = jnp.dot(a, w2_ref[...], preferred_element_type=F32)
        y = y * _sigmoid(y) * u
    if residual:
        y = y + r_ref[...]
    o_ref[...] = y.astype(o_ref.dtype)


def _dense(a, w, layer, *, tm, tn, g=None, residual=None, swiglu=False, out_dtype=F32, name):
    m, k = a.shape
    n = w.shape[2] // 2 if swiglu else w.shape[2]
    assert m % tm == 0 and n % tn == 0
    rms = g is not None
    stage = rms or a.dtype != BF16
    nj = n // tn
    in_specs = [pl.BlockSpec((tm, k), lambda i, j: (i, 0))]
    args = [a]
    if rms:
        in_specs.append(pl.BlockSpec((1, k), lambda i, j: (0, 0)))
        args.append(g.reshape(1, k))
    in_specs.append(pl.BlockSpec((None, k, tn), lambda i, j: (layer, 0, j)))
    args.append(w)
    if swiglu:
        in_specs.append(pl.BlockSpec((None, k, tn), lambda i, j: (layer, 0, j + nj)))
        args.append(w)
    if residual is not None:
        in_specs.append(pl.BlockSpec((tm, tn), lambda i, j: (i, j)))
        args.append(residual)
    return pl.pallas_call(
        functools.partial(_dense_kernel, rms=rms, swiglu=swiglu, residual=residual is not None, stage=stage),
        grid=(m // tm, nj),
        in_specs=in_specs,
        out_specs=pl.BlockSpec((tm, tn), lambda i, j: (i, j)),
        out_shape=jax.ShapeDtypeStruct((m, n), out_dtype),
        scratch_shapes=[pltpu.VMEM((tm, k), BF16)] if stage else [],
        compiler_params=_cparams("arbitrary", "arbitrary"),
        name=name,
    )(*args)


def _segsum64(x, ones_blk):
    parts = [
        _mm(x[:, LANES * i : LANES * (i + 1)], ones_blk, prec="a") for i in range(x.shape[1] // LANES)
    ]
    return jnp.concatenate(parts, axis=1)


def _head_ones():
    r = _iota2((LANES, LANES), 0) >> _log2(RW_HD)
    c = _iota2((LANES, LANES), 1) >> _log2(RW_HD)
    return _onehot(r == c)


def _rw_prep_kernel(
    pr_ref, pk_ref, pv_ref, pz_ref, sr_ref, sk_ref, sv_ref, sz_ref, vec_ref, muz_ref, lora_ref,
    r_o, k_o, v_o, kk_o, bb_o, lw_o, bonus_o, g_o,
    prev_r, prev_k, prev_v, prev_z,
):
    tb = pr_ref.shape[0]

    @pl.when(pl.program_id(1) == 0)
    def _():
        prev_r[...] = sr_ref[0]
        prev_k[...] = sk_ref[0]
        prev_v[...] = sv_ref[0]
        prev_z[...] = sz_ref[0]

    def shift(p_ref, prev, mu):
        p = p_ref[...]
        row = _iota2(p.shape, 0)
        p_prev = jnp.where(row == 0, prev[...], pltpu.roll(p, 1, 0))
        prev[...] = p[tb - 1 : tb, :]
        return p + mu * (p_prev - p)

    vec = vec_ref[...]
    w0, a0, k_k, k_a, r_k = vec[0:1], vec[1:2], vec[2:3], vec[3:4], vec[4:5]
    r = shift(pr_ref, prev_r, vec[5:6])
    k = shift(pk_ref, prev_k, vec[6:7])
    v = shift(pv_ref, prev_v, vec[7:8])
    z = shift(pz_ref, prev_z, muz_ref[...])

    lane = _iota2(z.shape, 1)
    in_w = lane < RW_DECAY_LORA
    in_a = (lane >= RW_DECAY_LORA) & (lane < RW_DECAY_LORA + RW_A_LORA)
    lora = lora_ref[...]
    lw_lin = _mm(jnp.where(in_w, jnp.tanh(z), 0.0), lora, prec=3)
    a_lin = _mm(jnp.where(in_a, z, 0.0), lora, prec=3)
    g = _mm(jnp.where(in_w | in_a, 0.0, _sigmoid(z)), lora, prec=3)

    lw = -jnp.exp(-_softplus(-(w0 + lw_lin)) - 0.5)
    a = _sigmoid(a0 + a_lin)
    ones_blk = _head_ones()
    kk = k * k_k
    kk = kk / jnp.maximum(jnp.sqrt(_segsum64(kk * kk, ones_blk)), 1e-12)
    k = k * (1.0 + (a - 1.0) * k_a)
    r_o[...] = r
    k_o[...] = k
    v_o[...] = v
    kk_o[...] = kk
    bb_o[...] = a * kk
    lw_o[...] = lw
    bonus_o[...] = _segsum64(r * k * r_k, ones_blk) * v
    g_o[...] = g


def _rw_prep(proj, col0, col_z, shift0, vec, muz, lora, *, nseq, t, tb, name):
    w = 512
    nt = t // tb
    row = lambda b, i: b * nt + i
    sr, sk, sv, sz = shift0
    in_specs = [
        pl.BlockSpec((tb, w), lambda b, i: (row(b, i), col0)),
        pl.BlockSpec((tb, w), lambda b, i: (row(b, i), col0 + 1)),
        pl.BlockSpec((tb, w), lambda b, i: (row(b, i), col0 + 2)),
        pl.BlockSpec((tb, 256), lambda b, i: (row(b, i), col_z)),
        pl.BlockSpec((1, 1, w), lambda b, i: (b, 0, 0)),
        pl.BlockSpec((1, 1, w), lambda b, i: (b, 0, 0)),
        pl.BlockSpec((1, 1, w), lambda b, i: (b, 0, 0)),
        pl.BlockSpec((1, 1, 256), lambda b, i: (b, 0, 0)),
        pl.BlockSpec((8, w), lambda b, i: (0, 0)),
        pl.BlockSpec((1, 256), lambda b, i: (0, 0)),
        pl.BlockSpec((256, w), lambda b, i: (0, 0)),
    ]
    out = jax.ShapeDtypeStruct((nseq * t, w), F32)
    return pl.pallas_call(
        _rw_prep_kernel,
        grid=(nseq, nt),
        in_specs=in_specs,
        out_specs=[pl.BlockSpec((tb, w), lambda b, i: (row(b, i), 0))] * 8,
        out_shape=[out] * 8,
        scratch_shapes=[pltpu.VMEM((1, w), F32)] * 3 + [pltpu.VMEM((1, 256), F32)],
        compiler_params=_cparams("arbitrary", "arbitrary"),
        name=name,
    )(proj, proj, proj, proj, sr, sk, sv, sz, vec, muz, lora)


RW_PREC_SOLVE = 1
RW_PREC_APPLY = 1


def _rw_scan_kernel(
    r_ref, k_ref, v_ref, kk_ref, bb_ref, lw_ref, bonus_ref, g_ref, ln_ref, s0_ref,
    o_ref, st_ref, a_sc, *, c, valid, nchunks,
):
    ci = pl.program_id(1)

    @pl.when(ci == 0)
    def _():
        a_sc[...] = s0_ref[...]

    hp = LANES // RW_HD
    n = hp * c
    ti = _iota2((n, n), 0)
    si = _iota2((n, n), 1)
    same_head = (ti >> _log2(c)) == (si >> _log2(c))
    incl = same_head & (si <= ti)
    strict = same_head & (si < ti)
    tri = _onehot(_iota2((c, c), 1) <= _iota2((c, c), 0))
    ri = _iota2((LANES, LANES), 0)
    cj = _iota2((LANES, LANES), 1)
    eye = ri == cj
    blockdiag = (ri >> _log2(RW_HD)) == (cj >> _log2(RW_HD))
    ones_blk = _onehot(blockdiag)
    own_lane = (_iota2((n, 1), 0) >> _log2(c)) == (_iota2((1, LANES), 1) >> _log2(RW_HD))
    live = None
    if valid < c:
        live = (_iota2((c, 1), 0) + ci * c) < valid
    mms = functools.partial(_mm, prec=RW_PREC_SOLVE)
    mma = functools.partial(_mm, prec=RW_PREC_APPLY)

    def dup(x):
        return jnp.concatenate([x] * hp, axis=0)

    def stack(x):
        return jnp.where(own_lane, dup(x), 0.0)

    def fold(x):
        out = x[0:c]
        for h in range(1, hp):
            out = out + x[h * c : (h + 1) * c]
        return out

    nsb, npair = r_ref.shape[0], r_ref.shape[2] // LANES
    chains = [(s, p) for s in range(nsb) for p in range(npair)]
    tiles = range(len(chains))

    def load(ref, masked):
        out = []
        for s, p in chains:
            x = ref[s, :, LANES * p : LANES * (p + 1)]
            out.append(jnp.where(live, x, 0.0) if (masked and live is not None) else x)
        return out

    r_l, v_l = load(r_ref, False), load(v_ref, False)
    k_l, kk_l, bb_l, lw_l = load(k_ref, True), load(kk_ref, True), load(bb_ref, True), load(lw_ref, True)
    lwc_l = [_mm(tri, lw, prec="b") for lw in lw_l]
    lwe_l = [lwc[c - 1 : c, :] for lwc in lwc_l]
    a_st_l, r_t_l, v_st_l, k_h_l, b_h_l, gram_l = [], [], [], [], [], []
    for p in tiles:
        lwc, lwe = lwc_l[p], lwe_l[p]
        e_neg = jnp.exp(-lwc)
        e_end = jnp.exp(lwe - lwc)
        a_st = stack(kk_l[p] * jnp.exp(lwc - lw_l[p]))
        r_t = r_l[p] * jnp.exp(lwc)
        a_st_l.append(a_st)
        r_t_l.append(r_t)
        v_st_l.append(stack(v_l[p]))
        k_h_l.append(k_l[p] * e_end)
        b_h_l.append(bb_l[p] * e_end)
        gram_l.append(mms(jnp.concatenate([a_st, stack(r_t)], axis=0),
                          jnp.concatenate([dup(bb_l[p] * e_neg), dup(k_l[p] * e_neg)], axis=0), NT))
    l_l = [jnp.where(strict, g[:n, :n], 0.0) for g in gram_l]
    mv_l = [mma(jnp.where(strict, g[:n, n:], 0.0), v_st) for g, v_st in zip(gram_l, v_st_l)]
    t_l = [_onehot(ti == si, F32) for _ in tiles]
    lev = 0
    while (1 << lev) < c:
        off = ((ti >> (lev + 1)) == (si >> (lev + 1))) & ((ti >> lev) != (si >> lev))
        tl_l = [mms(t, jnp.where(off, l, 0.0)) for t, l in zip(t_l, l_l)]
        t_l = [t - mms(tl, t) for t, tl in zip(t_l, tl_l)]
        lev += 1
    z_l = [mma(t, jnp.concatenate([a_st, mv], axis=1)) for t, a_st, mv in zip(t_l, a_st_l, mv_l)]
    qz_l = [mma(jnp.where(incl, g[n:, :n], 0.0), z) for g, z in zip(gram_l, z_l)]
    pv_l = [mma(jnp.where(incl, g[n:, n:], 0.0), v_st) for g, v_st in zip(gram_l, v_st_l)]
    bz_l = [mma(b_h, jnp.concatenate([fold(z[:, :LANES]), fold(z[:, LANES:])], axis=1), TN)
            for b_h, z in zip(b_h_l, z_l)]
    kv_l = [mma(k_h, v, TN) for k_h, v in zip(k_h_l, v_l)]
    ya_l = []
    for p in tiles:
        ry = r_t_l[p] - fold(qz_l[p][:, :LANES])
        g_mat = jnp.where(eye, jnp.exp(lwe_l[p]), 0.0) - jnp.where(blockdiag, bz_l[p][:, :LANES], 0.0)
        ya_l.append(mma(jnp.concatenate([ry, g_mat], axis=0), a_sc[chains[p]]))
    ys = [ya[:c] + fold(pv - qz[:, LANES:]) for ya, pv, qz in zip(ya_l, pv_l, qz_l)]
    states = [ya[c:] + jnp.where(blockdiag, kv - bz[:, LANES:], 0.0) for ya, kv, bz in zip(ya_l, kv_l, bz_l)]
    a_sc[...] = jnp.stack(states).reshape(a_sc.shape)
    y = jnp.concatenate([jnp.concatenate(ys[s * npair : (s + 1) * npair], axis=1) for s in range(nsb)], axis=0)
    mean = _segsum64(y, ones_blk) * (1.0 / RW_HD)
    d = y - mean
    var = _segsum64(d * d, ones_blk) * (1.0 / RW_HD)
    yn = (d * lax.rsqrt(var + RW_GN_EPS)).reshape(o_ref.shape)
    o_ref[...] = (yn * ln_ref[0:1, :] + ln_ref[1:2, :] + bonus_ref[...]) * g_ref[...]

    @pl.when(ci == nchunks - 1)
    def _():
        st_ref[...] = a_sc[...]


def _rw_scan(prep, ln, s0, *, nseq, t, c, valid, name):
    w = 512
    nchunks = -(-valid // c)
    npair = w // LANES
    nsb = RW_SEQS_PER_STEP if nseq % RW_SEQS_PER_STEP == 0 else 1
    blk = pl.BlockSpec((nsb, c, w), lambda b, i: (b, i, 0))
    st_spec = pl.BlockSpec((nsb, npair, LANES, LANES), lambda b, i: (b, 0, 0, 0))
    o, st = pl.pallas_call(
        functools.partial(_rw_scan_kernel, c=c, valid=valid, nchunks=nchunks),
        grid=(nseq // nsb, nchunks),
        in_specs=[blk] * 8 + [pl.BlockSpec((2, w), lambda b, i: (0, 0)), st_spec],
        out_specs=[blk, st_spec],
        out_shape=[
            jax.ShapeDtypeStruct((nseq, nchunks * c, w), F32),
            jax.ShapeDtypeStruct((nseq, npair, LANES, LANES), F32),
        ],
        scratch_shapes=[pltpu.VMEM((nsb, npair, LANES, LANES), F32)],
        compiler_params=_cparams("arbitrary", "arbitrary"),
        name=name,
    )(*[a.reshape(nseq, t, w) for a in prep], ln, s0)
    return o.reshape(nseq * nchunks * c, w), st


GL_PREC = 1


def _gla_kernel(*refs, tb, valid, nblocks, dk, nsb):
    seq_refs = [refs[5 * s : 5 * s + 5] for s in range(nsb)]
    ab_ref, abias_ref, ng_ref, s0_ref, o_ref, st_ref, s_sc = refs[5 * nsb :]
    bi = pl.program_id(1)

    @pl.when(bi == 0)
    def _():
        s_sc[...] = s0_ref[...]

    mm = functools.partial(_mm, prec=GL_PREC)
    ti = _iota2((tb, tb), 0)
    si = _iota2((tb, tb), 1)
    tri = _onehot(si <= ti)
    trow = _iota2((tb, 1), 0)
    lane_head = _iota2((1, LANES), 1) >> _log2(dk)
    heads_per_tile = LANES // dk
    own_lane = (_iota2((heads_per_tile * tb, 1), 0) >> _log2(tb)) == lane_head

    def dup(x):
        return jnp.concatenate([x] * heads_per_tile, axis=0)

    def stack(x):
        return jnp.where(own_lane, dup(x), 0.0)

    npair = seq_refs[0][0].shape[1] // LANES
    chains = [(s, p) for s in range(nsb) for p in range(npair)]
    q_l, k_l, la_l = [], [], []
    for s in range(nsb):
        q_ref, k_ref, _, xa_ref, _ = seq_refs[s]
        la_all = -_softplus(-(_mm(xa_ref[...], ab_ref[...], prec=3) + abias_ref[...])) * (1.0 / GL_TAU)
        k_all = k_ref[...]
        if valid < tb:
            live = (trow + bi * tb) < valid
            la_all = jnp.where(live, la_all, 0.0)
            k_all = jnp.where(live, k_all, 0.0)
        for p in range(npair):
            sl = slice(LANES * p, LANES * (p + 1))
            q_l.append(q_ref[:, sl] * (dk**-0.5))
            k_l.append(k_all[:, sl])
            la_l.append(la_all[:, sl])
    b_l = [_mm(tri, la, prec="b") for la in la_l]
    att_l = [jnp.where(dup(ti == si), mm(stack(q), k, NT), 0.0) for q, k in zip(q_l, k_l)]
    lev = 1
    while (1 << lev) <= tb:
        half = 1 << (lev - 1)
        sel = _onehot(si == ((ti >> lev) << lev) + (half - 1))
        upper = ((trow >> (lev - 1)) & 1) == 1
        same = dup((ti >> lev) == (si >> lev))
        bref_l = [_mm(sel, b, prec="b") for b in b_l]
        qs_l = [stack(jnp.where(upper, q * jnp.exp(jnp.minimum(b - br, 0.0)), 0.0)) for q, b, br in zip(q_l, b_l, bref_l)]
        ks_l = [jnp.where(upper, 0.0, k * jnp.exp(jnp.minimum(br - b, 0.0))) for k, b, br in zip(k_l, b_l, bref_l)]
        att_l = [att + jnp.where(same, mm(qs, ks, NT), 0.0) for att, qs, ks in zip(att_l, qs_l, ks_l)]
        lev += 1

    s_old_l = [s_sc[s, p] for s, p in chains]
    inter_l = [mm(stack(q * jnp.exp(b)), s_old, NT) for q, b, s_old in zip(q_l, b_l, s_old_l)]
    outs, states = [], []
    for i, (s, p) in enumerate(chains):
        b_end = b_l[i][tb - 1 : tb, :]
        ke = k_l[i] * jnp.exp(b_end - b_l[i])
        s_new = s_old_l[i] * jnp.exp(b_end)
        for h in range(heads_per_tile):
            hv = heads_per_tile * p + h
            v = seq_refs[s][2][:, LANES * hv : LANES * (hv + 1)]
            o = mm(att_l[i][h * tb : (h + 1) * tb], v) + inter_l[i][h * tb : (h + 1) * tb]
            s_new = s_new + mm(v, jnp.where(lane_head == h, ke, 0.0), TN)
            outs.append(o * lax.rsqrt(jnp.mean(o * o, axis=-1, keepdims=True) + NORM_EPS) * ng_ref[...])
        states.append(s_new)

    s_sc[...] = jnp.stack(states).reshape(s_sc.shape)
    per_seq = len(outs) // nsb
    for s in range(nsb):
        gg = seq_refs[s][4][...]
        o_ref[s] = jnp.concatenate(outs[s * per_seq : (s + 1) * per_seq], axis=1) * (gg * _sigmoid(gg))

    @pl.when(bi == nblocks - 1)
    def _():
        st_ref[...] = s_sc[...]


def _gla(proj, cols, ab, abias, ng, s0, *, nseq, t, tb, valid, name):
    cq, ck, cv, cxa, cgg = cols
    nb = t // tb
    nrun = -(-valid // tb)
    npair = 256 // LANES
    nsb = GL_SEQS_PER_STEP if nseq % GL_SEQS_PER_STEP == 0 else 1
    st_spec = pl.BlockSpec((nsb, npair, LANES, LANES), lambda b, i: (b, 0, 0, 0))

    def seq_specs(s):
        row = lambda b, i: (b * nsb + s) * nb + i
        return [
            pl.BlockSpec((tb, 256), lambda b, i: (row(b, i), cq)),
            pl.BlockSpec((tb, 256), lambda b, i: (row(b, i), ck)),
            pl.BlockSpec((tb, 512), lambda b, i: (row(b, i), cv)),
            pl.BlockSpec((tb, LANES), lambda b, i: (row(b, i), cxa)),
            pl.BlockSpec((tb, 512), lambda b, i: (row(b, i), cgg)),
        ]

    o, st = pl.pallas_call(
        functools.partial(_gla_kernel, tb=tb, valid=valid, nblocks=nrun, dk=256 // GL_HEADS, nsb=nsb),
        grid=(nseq // nsb, nrun),
        in_specs=[spec for s in range(nsb) for spec in seq_specs(s)]
        + [
            pl.BlockSpec((LANES, 256), lambda b, i: (0, 0)),
            pl.BlockSpec((1, 256), lambda b, i: (0, 0)),
            pl.BlockSpec((1, LANES), lambda b, i: (0, 0)),
            st_spec,
        ],
        out_specs=[pl.BlockSpec((nsb, tb, 512), lambda b, i: (b, i, 0)), st_spec],
        out_shape=[
            jax.ShapeDtypeStruct((nseq, nrun * tb, 512), F32),
            jax.ShapeDtypeStruct((nseq, npair, LANES, LANES), F32),
        ],
        scratch_shapes=[pltpu.VMEM((nsb, npair, LANES, LANES), F32)],
        compiler_params=_cparams("arbitrary", "arbitrary"),
        name=name,
    )(*([proj] * (5 * nsb)), ab, abias, ng, s0)
    return o.reshape(nseq * nrun * tb, 512), st


def _fx_prep_kernel(*refs, stacked):
    q_ref, k_ref, v_ref, fl_ref, qg_ref, kg_ref, bf_ref = refs[:7]
    qb_o, kf_o, vf_o, kb_o, vb_o, lf_o, c_o, carry = refs[9:] if stacked else refs[7:]
    tb = q_ref.shape[0]

    @pl.when(pl.program_id(1) == 0)
    def _():
        carry[...] = jnp.zeros_like(carry)

    def headnorm(x, gain):
        return x * lax.rsqrt(jnp.mean(x * x, axis=-1, keepdims=True) + NORM_EPS) * gain

    nh = q_ref.shape[1] // FX_HD
    for h in range(nh):
        sl = slice(FX_HD * h, FX_HD * (h + 1))
        qb_o[:, sl] = (headnorm(q_ref[:, sl], qg_ref[...]) * (FX_HD**-0.5 * LOG2E)).astype(BF16)
        kn = headnorm(k_ref[:, sl], kg_ref[...])
        kb_o[:, sl] = kn.astype(BF16)
        kf_o[pl.ds(h, tb, stride=nh), :] = kn
        vf_o[pl.ds(h, tb, stride=nh), :] = v_ref[:, sl]
    vb_o[...] = v_ref[...].astype(BF16)
    logf = -_softplus(-(fl_ref[...] + bf_ref[...]))
    lf_o[...] = logf
    tri = _onehot(_iota2((tb, tb), 1) <= _iota2((tb, tb), 0))
    c = _mm(tri, logf, prec="b") + carry[...]
    c_o[...] = c
    carry[...] = c[tb - 1 : tb, :]


def _fx_prep(proj, cols, qg, kg, bf, *, nseq, t, tb, name, stack=None):
    cq, ck, cv, cfl = cols
    nt = t // tb
    row = lambda b, i: b * nt + i
    w = 1024
    nh = w // FX_HD
    wide = pl.BlockSpec((tb, w), lambda b, i: (row(b, i), 0))
    thin = pl.BlockSpec((tb, LANES), lambda b, i: (row(b, i), 0))
    vecspec = pl.BlockSpec((1, LANES), lambda b, i: (0, 0))
    rows = nseq * t
    extra_in, extra_args, aliases = [], [], {}
    kv_spec = pl.BlockSpec((tb * nh, FX_HD), lambda b, i: (row(b, i), 0))
    kv_shape = jax.ShapeDtypeStruct((rows * nh, FX_HD), F32)
    if stack is not None:
        k_all, v_all, layer = stack
        extra_in = [pl.BlockSpec(memory_space=pl.ANY)] * 2
        extra_args = [k_all, v_all]
        aliases = {7: 1, 8: 2}
        kv_spec = pl.BlockSpec((None, None, tb * nh, FX_HD), lambda b, i: (b, layer, i, 0))
        kv_shape = jax.ShapeDtypeStruct(k_all.shape, F32)
    return pl.pallas_call(
        functools.partial(_fx_prep_kernel, stacked=stack is not None),
        grid=(nseq, nt),
        in_specs=[
            pl.BlockSpec((tb, w), lambda b, i: (row(b, i), cq)),
            pl.BlockSpec((tb, w), lambda b, i: (row(b, i), ck)),
            pl.BlockSpec((tb, w), lambda b, i: (row(b, i), cv)),
            pl.BlockSpec((tb, LANES), lambda b, i: (row(b, i), cfl)),
            vecspec, vecspec, vecspec,
        ] + extra_in,
        out_specs=[wide, kv_spec, kv_spec, wide, wide, thin, thin],
        out_shape=[
            jax.ShapeDtypeStruct((rows, w), BF16),
            kv_shape,
            kv_shape,
            jax.ShapeDtypeStruct((rows, w), BF16),
            jax.ShapeDtypeStruct((rows, w), BF16),
            jax.ShapeDtypeStruct((rows, LANES), F32),
            jax.ShapeDtypeStruct((rows, LANES), F32),
        ],
        scratch_shapes=[pltpu.VMEM((1, LANES), F32)],
        input_output_aliases=aliases,
        compiler_params=_cparams("arbitrary", "arbitrary"),
        name=name,
    )(proj, proj, proj, proj, qg, kg, bf, *extra_args)


def _with_ones(v):
    return jnp.concatenate([v, jnp.ones_like(v)], axis=1)


def _flash_kernel(qi_ref, kj_ref, q_ref, k_ref, v_ref, c_ref, cref_ref, o_ref, m_sc, acc_sc, *, tq, tk, hb):
    qi = qi_ref[pl.program_id(2)]
    kj = kj_ref[pl.program_id(2)]

    @pl.when(kj == 0)
    def _():
        m_sc[...] = jnp.full_like(m_sc, -jnp.inf)
        acc_sc[...] = jnp.zeros_like(acc_sc)

    def update(masked):
        heads = range(hb)
        sl = [slice(FX_HD * h, FX_HD * (h + 1)) for h in heads]
        s_l = [lax.dot_general(q_ref[:, sl[h]], k_ref[:, sl[h]], NT, preferred_element_type=F32) for h in heads]
        s_l = [s_l[h] + (cref_ref[0, h, :, 0:1] - c_ref[0, h]) * LOG2E for h in heads]
        if masked:
            keep = (kj * tk + _iota2((tq, tk), 1)) <= (qi * tq + _iota2((tq, tk), 0))
            s_l = [jnp.where(keep, s, -jnp.inf) for s in s_l]
        m_old = [m_sc[h] for h in heads]
        m_new = [jnp.maximum(m_old[h], jnp.max(s_l[h], axis=-1, keepdims=True)) for h in heads]
        p_l = [jnp.exp2(s_l[h] - pltpu.repeat(m_new[h], tk // LANES, axis=1)) for h in heads]
        pv_l = [jnp.dot(p_l[h].astype(BF16), _with_ones(v_ref[:, sl[h]]), preferred_element_type=F32) for h in heads]
        for h in heads:
            alpha = jnp.exp2(m_old[h] - m_new[h])
            acc_sc[h] = pltpu.repeat(alpha, 2, axis=1) * acc_sc[h] + pv_l[h]
            m_sc[h] = m_new[h]

    first_key, last_key = kj * tk, kj * tk + (tk - 1)
    first_query, last_query = qi * tq, qi * tq + (tq - 1)

    @pl.when(last_key <= first_query)
    def _():
        update(False)

    @pl.when(last_key > first_query)
    def _():
        update(True)

    @pl.when(first_key + tk > last_query)
    def _():
        for h in range(hb):
            acc = acc_sc[h]
            o_ref[:, FX_HD * h : FX_HD * (h + 1)] = acc[:, :FX_HD] / acc[:, FX_HD:]


def _flash(qb, kb, vb, c_row, *, nseq, t, tq, tk, name):
    nh = qb.shape[1] // FX_HD
    hb = FLASH_HEADS_PER_STEP if nh % FLASH_HEADS_PER_STEP == 0 else 1
    nq, nk = t // tq, t // tk
    pairs = [(i, j) for i in range(nq) for j in range(nk) if j * tk <= i * tq + tq - 1]
    qi_tab = jnp.array([i for i, _ in pairs], jnp.int32)
    kj_tab = jnp.array([j for _, j in pairs], jnp.int32)
    w = hb * FX_HD
    grid_spec = pltpu.PrefetchScalarGridSpec(
        num_scalar_prefetch=2,
        grid=(nseq, nh // hb, len(pairs)),
        in_specs=[
            pl.BlockSpec((tq, w), lambda b, h, p, qt, kt: (b * nq + qt[p], h)),
            pl.BlockSpec((tk, w), lambda b, h, p, qt, kt: (b * nk + kt[p], h)),
            pl.BlockSpec((tk, w), lambda b, h, p, qt, kt: (b * nk + kt[p], h)),
            pl.BlockSpec((1, hb, 1, tk), lambda b, h, p, qt, kt: (b, h, 0, kt[p])),
            pl.BlockSpec((1, hb, 1, LANES), lambda b, h, p, qt, kt: (b, h, 0, qt[p] * (tq // LANES))),
        ],
        out_specs=pl.BlockSpec((tq, w), lambda b, h, p, qt, kt: (b * nq + qt[p], h)),
        scratch_shapes=[pltpu.VMEM((hb, tq, LANES), F32), pltpu.VMEM((hb, tq, 2 * FX_HD), F32)],
    )
    return pl.pallas_call(
        functools.partial(_flash_kernel, tq=tq, tk=tk, hb=hb),
        grid_spec=grid_spec,
        out_shape=jax.ShapeDtypeStruct((nseq * t, nh * FX_HD), F32),
        compiler_params=_cparams("arbitrary", "arbitrary", "arbitrary"),
        name=name,
    )(qi_tab, kj_tab, qb, kb, vb, c_row, c_row)


def _decode_kernel(pt_ref, q_ref, kn_ref, vn_ref, cn_ref, *rest, nq, nh, pp):
    kc_refs, vc_refs, lf_refs = rest[0:pp], rest[pp : 2 * pp], rest[2 * pp : 3 * pp]
    o_ref, m_sc, acc_sc, carry = rest[3 * pp :]
    j = pl.program_id(1)
    heads = range(nh)
    q_l = [q_ref[:, FX_HD * h : FX_HD * (h + 1)] for h in heads]

    def attend(pages, causal):
        kv = [[load(h) for h in heads] for load, _ in pages]
        s = [[lax.dot_general(q_l[h], kv[i][h][0], NT, preferred_element_type=F32) + pages[i][1][h : h + 1, :]
              for h in heads] for i in range(len(pages))]
        if causal:
            keep = _iota2((nq, PAGE), 1) <= _iota2((nq, PAGE), 0)
            s = [[jnp.where(keep, x, -jnp.inf) for x in row] for row in s]
        for h in heads:
            m_old = m_sc[h]
            m_new = m_old
            for i in range(len(pages)):
                m_new = jnp.maximum(m_new, jnp.max(s[i][h], axis=-1, keepdims=True))
            acc = pltpu.repeat(jnp.exp2(m_old - m_new), 2, axis=1) * acc_sc[h]
            for i in range(len(pages)):
                p = jnp.exp2(s[i][h] - m_new)
                acc = acc + jnp.dot(p.astype(BF16), _with_ones(kv[i][h][1]), preferred_element_type=F32)
            acc_sc[h] = acc
            m_sc[h] = m_new

    @pl.when(j == 0)
    def _():
        m_sc[...] = jnp.full_like(m_sc, -jnp.inf)
        acc_sc[...] = jnp.zeros_like(acc_sc)
        carry[...] = jnp.zeros_like(carry)

        def new_rows(h):
            sl = slice(FX_HD * h, FX_HD * (h + 1))
            return kn_ref[:, sl], vn_ref[:, sl]

        attend([(new_rows, -cn_ref[0] * LOG2E)], True)

    @pl.when(j > 0)
    def _():
        after = _onehot(_iota2((PAGE, PAGE), 0) > _iota2((PAGE, PAGE), 1))
        both = jnp.concatenate([after, jnp.ones((PAGE, PAGE), BF16)], axis=1)
        run = carry[...]
        pages = []
        for i in range(pp):

            def cached_rows(h, i=i):
                rows = pl.ds(h, PAGE, stride=nh)
                return kc_refs[i][rows, :].astype(BF16), vc_refs[i][rows, :].astype(BF16)

            sums = _mm(lf_refs[i][...], both, prec="a")
            pages.append((cached_rows, (run + sums[:, :PAGE]) * LOG2E))
            run = run + sums[:, PAGE:]
        carry[...] = run
        attend(pages, False)

    @pl.when(j == pl.num_programs(1) - 1)
    def _():
        for h in heads:
            acc = acc_sc[h]
            o_ref[:, FX_HD * h : FX_HD * (h + 1)] = acc[:, :FX_HD] / acc[:, FX_HD:]


def _decode(qb, kb, vb, cn_t, cache_k, cache_v, cache_logf, page_table, layer, *, nq, name):
    nseq, npages = page_table.shape
    pool, depth, _, nh, hd = cache_k.shape
    w = nh * hd
    pp = DEC_PAGES_PER_STEP if npages % DEC_PAGES_PER_STEP == 0 else 1
    ck = cache_k.reshape(pool, depth, PAGE * nh, hd)
    cv = cache_v.reshape(pool, depth, PAGE * nh, hd)
    lf_t = jnp.swapaxes(cache_logf, 2, 3)

    def page_of(i):
        def index(b, j, pt):
            back = (jnp.maximum(j, 1) - 1) * pp + i
            return (pt[b * npages + (npages - 1 - back)], layer, 0, 0)

        return index

    grid_spec = pltpu.PrefetchScalarGridSpec(
        num_scalar_prefetch=1,
        grid=(nseq, 1 + npages // pp),
        in_specs=[
            pl.BlockSpec((nq, w), lambda b, j, pt: (b * (PAGE // nq), 0)),
            pl.BlockSpec((PAGE, w), lambda b, j, pt: (b, 0)),
            pl.BlockSpec((PAGE, w), lambda b, j, pt: (b, 0)),
            pl.BlockSpec((1, nh, PAGE), lambda b, j, pt: (b, 0, 0)),
        ]
        + [pl.BlockSpec((None, None, PAGE * nh, hd), page_of(i)) for i in range(pp)] * 2
        + [pl.BlockSpec((None, None, nh, PAGE), page_of(i)) for i in range(pp)],
        out_specs=pl.BlockSpec((nq, w), lambda b, j, pt: (b, 0)),
        scratch_shapes=[
            pltpu.VMEM((nh, nq, LANES), F32),
            pltpu.VMEM((nh, nq, 2 * hd), F32),
            pltpu.VMEM((nh, PAGE), F32),
        ],
    )
    return pl.pallas_call(
        functools.partial(_decode_kernel, nq=nq, nh=nh, pp=pp),
        grid_spec=grid_spec,
        out_shape=jax.ShapeDtypeStruct((nseq * nq, w), F32),
        compiler_params=_cparams("arbitrary", "arbitrary"),
        name=name,
    )(page_table.reshape(-1), qb, kb, vb, cn_t, *([ck] * pp), *([cv] * pp), *([lf_t] * pp))


RW_CHUNK = 64
RW_SEQS_PER_STEP = 2
GL_BLOCK = 128
GL_SEQS_PER_STEP = 2
FLASH_HEADS_PER_STEP = 2
DEC_Q_ROWS = 16
DEC_PAGES_PER_STEP = 8
TM = 768
COL_FX_Q, COL_FX_K, COL_FX_V = 0, 1024, 2048
COL_RW_R = 3072
COL_GL_V, COL_GL_GG = 4608, 5120
COL_GL_Q, COL_GL_K = 5632, 5888
COL_RW_Z = 6144
COL_FX_FL = 6400
COL_GL_XA = 6528
COLS_PAD = 6656


def _pad_cols(x, width):
    return jnp.pad(x, [(0, 0)] * (x.ndim - 1) + [(0, width - x.shape[-1])])


def _relayout_in(w, rw_w, fx_w, gl_kw, gl_vw, nh_fx):
    rw_z = RW_DECAY_LORA + RW_A_LORA + RW_G_LORA
    o = 0
    rw_rkv = w[..., o : o + 3 * rw_w]; o += 3 * rw_w
    rw_zz = w[..., o : o + rw_z]; o += rw_z
    fx_qkv = w[..., o : o + 3 * fx_w]; o += 3 * fx_w
    fx_fl = w[..., o : o + nh_fx]; o += nh_fx
    gl_qk = w[..., o : o + 2 * gl_kw]; o += 2 * gl_kw
    gl_v = w[..., o : o + gl_vw]; o += gl_vw
    gl_xa = w[..., o : o + GL_GATE_LORA]; o += GL_GATE_LORA
    gl_gg = w[..., o : o + gl_vw]; o += gl_vw
    assert o == w.shape[-1]
    return jnp.concatenate(
        [fx_qkv, rw_rkv, gl_v, gl_gg, gl_qk, _pad_cols(rw_zz, 256), _pad_cols(fx_fl, LANES), _pad_cols(gl_xa, LANES)],
        axis=-1,
    )


def _mixers(proj, layer, lp, states, paged, *, nseq, t, valid, tag, kv_stack=None):
    (shift0, rw_s0, gl_s0) = states
    tb = min(t, 256)
    prep = _rw_prep(proj, COL_RW_R // 512, COL_RW_Z // 256, shift0, lp["rw_vec"], lp["rw_muz"], lp["rw_lora"],
                    nseq=nseq, t=t, tb=tb, name=f"rw_prep_{tag}")
    o_rw, rw_state = _rw_scan(prep, lp["rw_ln"], rw_s0, nseq=nseq, t=t, c=min(t, RW_CHUNK), valid=valid,
                              name=f"rw_scan_{tag}")
    o_gl, gl_state = _gla(
        proj, (COL_GL_Q // 256, COL_GL_K // 256, COL_GL_V // 512, COL_GL_XA // LANES, COL_GL_GG // 512),
        lp["gl_ab"], lp["gl_abias"], lp["gl_ng"], gl_s0, nseq=nseq, t=t, tb=min(t, GL_BLOCK), valid=valid,
        name=f"gla_{tag}")
    qb, kf, vf, kb, vb, logf, c = _fx_prep(
        proj, (COL_FX_Q // 1024, COL_FX_K // 1024, COL_FX_V // 1024, COL_FX_FL // LANES),
        lp["fx_qg"], lp["fx_kg"], lp["fx_bf"], nseq=nseq, t=t, tb=tb, name=f"fx_prep_{tag}",
        stack=None if kv_stack is None else (*kv_stack, layer))
    nh = kb.shape[1] // FX_HD
    c_t = jnp.transpose(c.reshape(nseq, t, LANES)[:, :, :nh], (0, 2, 1))
    if paged is None:
        blk = min(t, 512)
        o_fx = _flash(qb, kb, vb, c_t[:, :, None, :], nseq=nseq, t=t, tq=blk, tk=blk, name=f"flash_{tag}")
    else:
        cache_k, cache_v, cache_logf, page_table = paged
        nq = DEC_Q_ROWS
        assert t == PAGE and valid <= nq
        o_fx = _decode(qb, kb, vb, c_t, cache_k, cache_v, cache_logf, page_table, layer, nq=nq, name=f"decode_{tag}")
        o_fx = o_fx.reshape(nseq, nq, -1)[:, :valid].reshape(nseq * valid, -1)
    return o_rw, o_fx, o_gl, kf, logf, rw_state, gl_state, vf


def kernel(x_prompt, x_sample, cache_k, cache_v, cache_logf, state_rwkv, state_rwkv_shift, state_gla,
           page_table, attn_norm_g, w_in, rw_mu, rw_w_b, rw_w0, rw_a_b, rw_a0, rw_g_b, rw_k_k, rw_k_a,
           rw_r_k, rw_lnx_g, rw_lnx_b, fx_qn_g, fx_kn_g, fx_b_f, gl_alpha_b, gl_alpha_bias, gl_norm_g,
           w_out, ffn_norm_g, w_gu, w_down):
    nb, seq, d_model = x_prompt.shape
    ns, dseq, _ = x_sample.shape
    depth = w_in.shape[0]
    rw_w = rw_w0.shape[1]
    rw_heads = rw_w // RW_HD
    nh_fx = fx_b_f.shape[1]
    fx_w = nh_fx * FX_HD
    gl_kw = gl_alpha_b.shape[2]
    gl_dk = gl_kw // GL_HEADS
    gl_dv = gl_norm_g.shape[1]
    gl_vw = GL_HEADS * gl_dv
    rw_cols = 3 * rw_w + RW_DECAY_LORA + RW_A_LORA + RW_G_LORA
    assert (rw_w, fx_w, gl_kw, gl_vw, FX_HD, gl_dv) == (512, 1024, 256, 512, 128, 128)
    n_prompt = nb * seq
    n_sample = ns * dseq
    rows = n_prompt + n_sample
    tm = TM if rows >= TM else 64
    rows_pad = -(-rows // tm) * tm
    relayout = functools.partial(_relayout_in, rw_w=rw_w, fx_w=fx_w, gl_kw=gl_kw, gl_vw=gl_vw, nh_fx=nh_fx)

    w_in_p = relayout(w_in).astype(BF16)
    w_out_b, w_gu_b, w_down_b = w_out.astype(BF16), w_gu.astype(BF16), w_down.astype(BF16)
    rw_mu_r, rw_mu_k, rw_mu_v, rw_mu_z = (
        rw_mu[:, :rw_w], rw_mu[:, rw_w : 2 * rw_w], rw_mu[:, 2 * rw_w : 3 * rw_w], rw_mu[:, 3 * rw_w :])
    layers = []
    for l in range(depth):
        layers.append(dict(
            rw_vec=jnp.stack([rw_w0[l], rw_a0[l], rw_k_k[l], rw_k_a[l], rw_r_k[l].reshape(-1),
                              rw_mu_r[l], rw_mu_k[l], rw_mu_v[l]]),
            rw_muz=_pad_cols(rw_mu_z[l][None], 256),
            rw_lora=jnp.pad(jnp.concatenate([rw_w_b[l], rw_a_b[l], rw_g_b[l]], axis=0),
                            ((0, 256 - (rw_cols - 3 * rw_w)), (0, 0))),
            rw_ln=jnp.stack([rw_lnx_g[l], rw_lnx_b[l]]),
            fx_qg=fx_qn_g[l][None], fx_kg=fx_kn_g[l][None], fx_bf=_pad_cols(fx_b_f[l][None], LANES),
            gl_ab=jnp.pad(gl_alpha_b[l], ((0, LANES - GL_GATE_LORA), (0, 0))),
            gl_abias=gl_alpha_bias[l][None], gl_ng=gl_norm_g[l][None],
        ))

    def rw_state_in(s):
        n = s.shape[0]
        a = jnp.swapaxes(s, -1, -2).reshape(n, rw_heads // 2, 2, RW_HD, RW_HD)
        z = jnp.zeros_like(a[:, :, 0])
        top = jnp.concatenate([a[:, :, 0], z], axis=-1)
        bot = jnp.concatenate([z, a[:, :, 1]], axis=-1)
        return jnp.concatenate([top, bot], axis=-2)

    def rw_state_out(a):
        n = a.shape[0]
        h0 = a[:, :, :RW_HD, :RW_HD]
        h1 = a[:, :, RW_HD:, RW_HD:]
        return jnp.swapaxes(jnp.stack([h0, h1], axis=2).reshape(n, rw_heads, RW_HD, RW_HD), -1, -2)

    def gl_state_in(s):
        n = s.shape[0]
        a = jnp.swapaxes(s, -1, -2).reshape(n, GL_HEADS // 2, 2, gl_dv, gl_dk)
        return jnp.transpose(a, (0, 1, 3, 2, 4)).reshape(n, GL_HEADS // 2, gl_dv, 2 * gl_dk)

    def gl_state_out(a):
        n = a.shape[0]
        a = jnp.transpose(a.reshape(n, GL_HEADS // 2, gl_dv, 2, gl_dk), (0, 1, 3, 2, 4))
        return jnp.swapaxes(a.reshape(n, GL_HEADS, gl_dv, gl_dk), -1, -2)

    def shift_in(s):
        return (s[:, None, :rw_w], s[:, None, rw_w : 2 * rw_w], s[:, None, 2 * rw_w : 3 * rw_w],
                _pad_cols(s[:, None, 3 * rw_w :], 256))

    def shift_out(proj_rows):
        return jnp.concatenate([proj_rows[:, COL_RW_R : COL_RW_R + 3 * rw_w],
                                proj_rows[:, COL_RW_Z : COL_RW_Z + rw_cols - 3 * rw_w]], axis=-1)

    x = jnp.concatenate([x_prompt.reshape(n_prompt, d_model), x_sample.reshape(n_sample, d_model),
                         jnp.zeros((rows_pad - rows, d_model), x_prompt.dtype)], axis=0)
    t_s = PAGE
    p_states = (shift_in(jnp.zeros((nb, rw_cols), F32)), jnp.zeros((nb, rw_heads // 2, LANES, LANES), F32),
                jnp.zeros((nb, GL_HEADS // 2, LANES, LANES), F32))
    outs = {k: [] for k in ("plf", "prw", "psh", "pgl", "sk", "sv", "slf", "srw", "ssh", "sgl")}
    pk_all = jnp.zeros((nb, depth, seq * nh_fx, FX_HD), F32)
    pv_all = jnp.zeros((nb, depth, seq * nh_fx, FX_HD), F32)
    for l in range(depth):
        lp = layers[l]
        proj = _dense(x, w_in_p, l, tm=tm, tn=512, g=attn_norm_g[l], name=f"w_in_{l}")
        proj_s = jnp.pad(proj[n_prompt:rows].reshape(ns, dseq, COLS_PAD), ((0, 0), (0, t_s - dseq), (0, 0)))
        proj_s = proj_s.reshape(ns * t_s, COLS_PAD)
        s_states = (shift_in(state_rwkv_shift[:, l]), rw_state_in(state_rwkv[:, l]), gl_state_in(state_gla[:, l]))
        po = _mixers(proj, l, lp, p_states, None, nseq=nb, t=seq, valid=seq, tag=f"p{l}", kv_stack=(pk_all, pv_all))
        pk_all, pv_all = po[3], po[7]
        so = _mixers(proj_s, l, lp, s_states, (cache_k, cache_v, cache_logf, page_table),
                     nseq=ns, t=t_s, valid=dseq, tag=f"s{l}")

        def unpad(a):
            return a.reshape(ns, a.shape[0] // ns, -1)[:, :dseq].reshape(n_sample, -1)

        mix = jnp.concatenate([
            jnp.concatenate([po[0], po[1], po[2]], axis=1),
            jnp.concatenate([unpad(so[0]), so[1], unpad(so[2])], axis=1),
            jnp.zeros((rows_pad - rows, d_model), F32)], axis=0)
        x = _dense(mix, w_out_b, l, tm=tm, tn=512, residual=x, name=f"w_out_{l}")
        act = _dense(x, w_gu_b, l, tm=tm, tn=512, g=ffn_norm_g[l], swiglu=True, out_dtype=BF16, name=f"w_gu_{l}")
        x = _dense(act, w_down_b, l, tm=tm, tn=256, residual=x, name=f"w_down_{l}")

        outs["plf"].append(po[4][:, :nh_fx].reshape(nb, seq, nh_fx))
        outs["prw"].append(rw_state_out(po[5]))
        outs["psh"].append(shift_out(jnp.stack([proj[b * seq + seq - 1] for b in range(nb)])))
        outs["pgl"].append(gl_state_out(po[6]))
        outs["sk"].append(so[3].reshape(ns, t_s, nh_fx, FX_HD)[:, :dseq])
        outs["sv"].append(so[7].reshape(ns, t_s, nh_fx, FX_HD)[:, :dseq])
        outs["slf"].append(unpad(so[4])[:, :nh_fx].reshape(ns, dseq, nh_fx))
        outs["srw"].append(rw_state_out(so[5]))
        outs["ssh"].append(shift_out(proj[n_prompt:rows].reshape(ns, dseq, COLS_PAD)[:, dseq - 1]))
        outs["sgl"].append(gl_state_out(so[6]))

    st = {k: jnp.stack(v, axis=1) for k, v in outs.items()}
    y_prompt = x[:n_prompt].reshape(nb, seq, d_model)
    y_sample = x[n_prompt:rows].reshape(ns, dseq, d_model)
    kv_shape = (nb, depth, seq, nh_fx, FX_HD)
    return (y_prompt, y_sample, pk_all.reshape(kv_shape), pv_all.reshape(kv_shape), st["plf"], st["prw"], st["psh"], st["pgl"],
            st["sk"], st["sv"], st["slf"], st["srw"], st["ssh"], st["sgl"])
```

```python
import functools

import jax
import jax.numpy as jnp
from jax import lax
from jax.experimental import pallas as pl
from jax.experimental.pallas import tpu as pltpu

F32 = jnp.float32
BF16 = jnp.bfloat16

RW_HD = 64
RW_DECAY_LORA = 32
RW_A_LORA = 32
RW_G_LORA = 96
RW_GN_EPS = 64e-5
FX_HD = 128
GL_HEADS = 4
GL_GATE_LORA = 16
GL_TAU = 16.0
NORM_EPS = 1e-6
PAGE = 128
LOG2E = 1.4426950408889634

LANES = 128
VMEM_LIMIT = 56 * 1024 * 1024

NN = (((1,), (0,)), ((), ()))
NT = (((1,), (1,)), ((), ()))
TN = (((0,), (0,)), ((), ()))


def _split2(x):
    hi = x.astype(BF16)
    lo = (x - hi.astype(F32)).astype(BF16)
    return hi, lo


def _split3(x):
    hi = x.astype(BF16)
    r = x - hi.astype(F32)
    mid = r.astype(BF16)
    lo = (r - mid.astype(F32)).astype(BF16)
    return hi, mid, lo


def _mm(a, b, dims=NN, prec=1):
    dot = functools.partial(lax.dot_general, dimension_numbers=dims, preferred_element_type=F32)
    if prec == 1:
        return dot(a.astype(BF16), b.astype(BF16))
    if prec == 3:
        ah, al = _split2(a)
        bh, bl = _split2(b)
        return dot(ah, bh) + (dot(ah, bl) + dot(al, bh))
    if prec == "a":
        bb = b.astype(BF16)
        h, m, l = _split3(a)
        return dot(h, bb) + (dot(m, bb) + dot(l, bb))
    if prec == "b":
        ab = a.astype(BF16)
        h, m, l = _split3(b)
        return dot(ab, h) + (dot(ab, m) + dot(ab, l))
    raise ValueError(prec)


def _sigmoid(x):
    return 1.0 / (1.0 + jnp.exp(-x))


def _softplus(x):
    return jnp.maximum(x, 0.0) + jnp.log1p(jnp.exp(-jnp.abs(x)))


def _iota2(shape, axis):
    return lax.broadcasted_iota(jnp.int32, shape, axis)


def _log2(n):
    assert n & (n - 1) == 0
    return n.bit_length() - 1


def _onehot(mask, dtype=BF16):
    return jnp.where(mask, 1.0, 0.0).astype(dtype)


def _cparams(*sem):
    return pltpu.CompilerParams(dimension_semantics=sem, vmem_limit_bytes=VMEM_LIMIT)


def _dense_kernel(*refs, rms, swiglu, residual, stage):
    it = iter(refs)
    a_ref = next(it)
    g_ref = next(it) if rms else None
    w_ref = next(it)
    w2_ref = next(it) if swiglu else None
    r_ref = next(it) if residual else None
    o_ref = next(it)
    a_sc = next(it) if stage else None

    if stage:
        @pl.when(pl.program_id(1) == 0)
        def _():
            a = a_ref[...].astype(F32)
            if rms:
                a = a * lax.rsqrt(jnp.mean(a * a, axis=-1, keepdims=True) + NORM_EPS) * g_ref[...]
            a_sc[...] = a.astype(BF16)

        a = a_sc[...]
    else:
        a = a_ref[...]
    y = jnp.dot(a, w_ref[...], preferred_element_type=F32)
    if swiglu:
        u = jnp.dot(a, w2_ref[...], preferred_element_type=F32)
        y = y * _sigmoid(y) * u
    if residual:
        y = y + r_ref[...]
    o_ref[...] = y.astype(o_ref.dtype)


def _dense(a, w, layer, *, tm, tn, g=None, residual=None, swiglu=False, out_dtype=F32, name):
    m, k = a.shape
    n = w.shape[2] // 2 if swiglu else w.shape[2]
    assert m % tm == 0 and n % tn == 0
    rms = g is not None
    stage = rms or a.dtype != BF16
    nj = n // tn
    in_specs = [pl.BlockSpec((tm, k), lambda i, j: (i, 0))]
    args = [a]
    if rms:
        in_specs.append(pl.BlockSpec((1, k), lambda i, j: (0, 0)))
        args.append(g.reshape(1, k))
    in_specs.append(pl.BlockSpec((None, k, tn), lambda i, j: (layer, 0, j)))
    args.append(w)
    if swiglu:
        in_specs.append(pl.BlockSpec((None, k, tn), lambda i, j: (layer, 0, j + nj)))
        args.append(w)
    if residual is not None:
        in_specs.append(pl.BlockSpec((tm, tn), lambda i, j: (i, j)))
        args.append(residual)
    return pl.pallas_call(
        functools.partial(_dense_kernel, rms=rms, swiglu=swiglu, residual=residual is not None, stage=stage),
        grid=(m // tm, nj),
        in_specs=in_specs,
        out_specs=pl.BlockSpec((tm, tn), lambda i, j: (i, j)),
        out_shape=jax.ShapeDtypeStruct((m, n), out_dtype),
        scratch_shapes=[pltpu.VMEM((tm, k), BF16)] if stage else [],
        compiler_params=_cparams("arbitrary", "arbitrary"),
        name=name,
    )(*args)


def _segsum64(x, ones_blk):
    parts = [
        _mm(x[:, LANES * i : LANES * (i + 1)], ones_blk, prec="a") for i in range(x.shape[1] // LANES)
    ]
    return jnp.concatenate(parts, axis=1)


def _head_ones():
    r = _iota2((LANES, LANES), 0) >> _log2(RW_HD)
    c = _iota2((LANES, LANES), 1) >> _log2(RW_HD)
    return _onehot(r == c)


def _rw_prep_kernel(
    pr_ref, pk_ref, pv_ref, pz_ref, sr_ref, sk_ref, sv_ref, sz_ref, vec_ref, muz_ref, lora_ref,
    r_o, k_o, v_o, kk_o, bb_o, lw_o, bonus_o, g_o,
    prev_r, prev_k, prev_v, prev_z,
):
    tb = pr_ref.shape[0]

    @pl.when(pl.program_id(1) == 0)
    def _():
        prev_r[...] = sr_ref[0]
        prev_k[...] = sk_ref[0]
        prev_v[...] = sv_ref[0]
        prev_z[...] = sz_ref[0]

    def shift(p_ref, prev, mu):
        p = p_ref[...]
        row = _iota2(p.shape, 0)
        p_prev = jnp.where(row == 0, prev[...], pltpu.roll(p, 1, 0))
        prev[...] = p[tb - 1 : tb, :]
        return p + mu * (p_prev - p)

    vec = vec_ref[...]
    w0, a0, k_k, k_a, r_k = vec[0:1], vec[1:2], vec[2:3], vec[3:4], vec[4:5]
    r = shift(pr_ref, prev_r, vec[5:6])
    k = shift(pk_ref, prev_k, vec[6:7])
    v = shift(pv_ref, prev_v, vec[7:8])
    z = shift(pz_ref, prev_z, muz_ref[...])

    lane = _iota2(z.shape, 1)
    in_w = lane < RW_DECAY_LORA
    in_a = (lane >= RW_DECAY_LORA) & (lane < RW_DECAY_LORA + RW_A_LORA)
    lora = lora_ref[...]
    lw_lin = _mm(jnp.where(in_w, jnp.tanh(z), 0.0), lora, prec=3)
    a_lin = _mm(jnp.where(in_a, z, 0.0), lora, prec=3)
    g = _mm(jnp.where(in_w | in_a, 0.0, _sigmoid(z)), lora, prec=3)

    lw = -jnp.exp(-_softplus(-(w0 + lw_lin)) - 0.5)
    a = _sigmoid(a0 + a_lin)
    ones_blk = _head_ones()
    kk = k * k_k
    kk = kk / jnp.maximum(jnp.sqrt(_segsum64(kk * kk, ones_blk)), 1e-12)
    k = k * (1.0 + (a - 1.0) * k_a)
    r_o[...] = r
    k_o[...] = k
    v_o[...] = v
    kk_o[...] = kk
    bb_o[...] = a * kk
    lw_o[...] = lw
    bonus_o[...] = _segsum64(r * k * r_k, ones_blk) * v
    g_o[...] = g


def _rw_prep(proj, col0, col_z, shift0, vec, muz, lora, *, nseq, t, tb, name):
    w = 512
    nt = t // tb
    row = lambda b, i: b * nt + i
    sr, sk, sv, sz = shift0
    in_specs = [
        pl.BlockSpec((tb, w), lambda b, i: (row(b, i), col0)),
        pl.BlockSpec((tb, w), lambda b, i: (row(b, i), col0 + 1)),
        pl.BlockSpec((tb, w), lambda b, i: (row(b, i), col0 + 2)),
        pl.BlockSpec((tb, 256), lambda b, i: (row(b, i), col_z)),
        pl.BlockSpec((1, 1, w), lambda b, i: (b, 0, 0)),
        pl.BlockSpec((1, 1, w), lambda b, i: (b, 0, 0)),
        pl.BlockSpec((1, 1, w), lambda b, i: (b, 0, 0)),
        pl.BlockSpec((1, 1, 256), lambda b, i: (b, 0, 0)),
        pl.BlockSpec((8, w), lambda b, i: (0, 0)),
        pl.BlockSpec((1, 256), lambda b, i: (0, 0)),
        pl.BlockSpec((256, w), lambda b, i: (0, 0)),
    ]
    out = jax.ShapeDtypeStruct((nseq * t, w), F32)
    return pl.pallas_call(
        _rw_prep_kernel,
        grid=(nseq, nt),
        in_specs=in_specs,
        out_specs=[pl.BlockSpec((tb, w), lambda b, i: (row(b, i), 0))] * 8,
        out_shape=[out] * 8,
        scratch_shapes=[pltpu.VMEM((1, w), F32)] * 3 + [pltpu.VMEM((1, 256), F32)],
        compiler_params=_cparams("arbitrary", "arbitrary"),
        name=name,
    )(proj, proj, proj, proj, sr, sk, sv, sz, vec, muz, lora)


RW_PREC_SOLVE = 1
RW_PREC_APPLY = 1


def _rw_scan_kernel(
    r_ref, k_ref, v_ref, kk_ref, bb_ref, lw_ref, bonus_ref, g_ref, ln_ref, s0_ref,
    o_ref, st_ref, a_sc, *, c, valid, nchunks,
):
    ci = pl.program_id(1)

    @pl.when(ci == 0)
    def _():
        a_sc[...] = s0_ref[...]

    hp = LANES // RW_HD
    n = hp * c
    ti = _iota2((n, n), 0)
    si = _iota2((n, n), 1)
    same_head = (ti >> _log2(c)) == (si >> _log2(c))
    incl = same_head & (si <= ti)
    strict = same_head & (si < ti)
    tri = _onehot(_iota2((c, c), 1) <= _iota2((c, c), 0))
    ri = _iota2((LANES, LANES), 0)
    cj = _iota2((LANES, LANES), 1)
    eye = ri == cj
    blockdiag = (ri >> _log2(RW_HD)) == (cj >> _log2(RW_HD))
    ones_blk = _onehot(blockdiag)
    own_lane = (_iota2((n, 1), 0) >> _log2(c)) == (_iota2((1, LANES), 1) >> _log2(RW_HD))
    live = None
    if valid < c:
        live = (_iota2((c, 1), 0) + ci * c) < valid
    mms = functools.partial(_mm, prec=RW_PREC_SOLVE)
    mma = functools.partial(_mm, prec=RW_PREC_APPLY)

    def dup(x):
        return jnp.concatenate([x] * hp, axis=0)

    def stack(x):
        return jnp.where(own_lane, dup(x), 0.0)

    def fold(x):
        out = x[0:c]
        for h in range(1, hp):
            out = out + x[h * c : (h + 1) * c]
        return out

    nsb, npair = r_ref.shape[0], r_ref.shape[2] // LANES
    chains = [(s, p) for s in range(nsb) for p in range(npair)]
    tiles = range(len(chains))

    def load(ref, masked):
        out = []
        for s, p in chains:
            x = ref[s, :, LANES * p : LANES * (p + 1)]
            out.append(jnp.where(live, x, 0.0) if (masked and live is not None) else x)
        return out

    r_l, v_l = load(r_ref, False), load(v_ref, False)
    k_l, kk_l, bb_l, lw_l = load(k_ref, True), load(kk_ref, True), load(bb_ref, True), load(lw_ref, True)
    lwc_l = [_mm(tri, lw, prec="b") for lw in lw_l]
    lwe_l = [lwc[c - 1 : c, :] for lwc in lwc_l]
    a_st_l, r_t_l, v_st_l, k_h_l, b_h_l, gram_l = [], [], [], [], [], []
    for p in tiles:
        lwc, lwe = lwc_l[p], lwe_l[p]
        e_neg = jnp.exp(-lwc)
        e_end = jnp.exp(lwe - lwc)
        a_st = stack(kk_l[p] * jnp.exp(lwc - lw_l[p]))
        r_t = r_l[p] * jnp.exp(lwc)
        a_st_l.append(a_st)
        r_t_l.append(r_t)
        v_st_l.append(stack(v_l[p]))
        k_h_l.append(k_l[p] * e_end)
        b_h_l.append(bb_l[p] * e_end)
        gram_l.append(mms(jnp.concatenate([a_st, stack(r_t)], axis=0),
                          jnp.concatenate([dup(bb_l[p] * e_neg), dup(k_l[p] * e_neg)], axis=0), NT))
    l_l = [jnp.where(strict, g[:n, :n], 0.0) for g in gram_l]
    mv_l = [mma(jnp.where(strict, g[:n, n:], 0.0), v_st) for g, v_st in zip(gram_l, v_st_l)]
    t_l = [_onehot(ti == si, F32) for _ in tiles]
    lev = 0
    while (1 << lev) < c:
        off = ((ti >> (lev + 1)) == (si >> (lev + 1))) & ((ti >> lev) != (si >> lev))
        tl_l = [mms(t, jnp.where(off, l, 0.0)) for t, l in zip(t_l, l_l)]
        t_l = [t - mms(tl, t) for t, tl in zip(t_l, tl_l)]
        lev += 1
    z_l = [mma(t, jnp.concatenate([a_st, mv], axis=1)) for t, a_st, mv in zip(t_l, a_st_l, mv_l)]
    qz_l = [mma(jnp.where(incl, g[n:, :n], 0.0), z) for g, z in zip(gram_l, z_l)]
    pv_l = [mma(jnp.where(incl, g[n:, n:], 0.0), v_st) for g, v_st in zip(gram_l, v_st_l)]
    bz_l = [mma(b_h, jnp.concatenate([fold(z[:, :LANES]), fold(z[:, LANES:])], axis=1), TN)
            for b_h, z in zip(b_h_l, z_l)]
    kv_l = [mma(k_h, v, TN) for k_h, v in zip(k_h_l, v_l)]
    ya_l = []
    for p in tiles:
        ry = r_t_l[p] - fold(qz_l[p][:, :LANES])
        g_mat = jnp.where(eye, jnp.exp(lwe_l[p]), 0.0) - jnp.where(blockdiag, bz_l[p][:, :LANES], 0.0)
        ya_l.append(mma(jnp.concatenate([ry, g_mat], axis=0), a_sc[chains[p]]))
    ys = [ya[:c] + fold(pv - qz[:, LANES:]) for ya, pv, qz in zip(ya_l, pv_l, qz_l)]
    states = [ya[c:] + jnp.where(blockdiag, kv - bz[:, LANES:], 0.0) for ya, kv, bz in zip(ya_l, kv_l, bz_l)]
    a_sc[...] = jnp.stack(states).reshape(a_sc.shape)
    y = jnp.concatenate([jnp.concatenate(ys[s * npair : (s + 1) * npair], axis=1) for s in range(nsb)], axis=0)
    mean = _segsum64(y, ones_blk) * (1.0 / RW_HD)
    d = y - mean
    var = _segsum64(d * d, ones_blk) * (1.0 / RW_HD)
    yn = (d * lax.rsqrt(var + RW_GN_EPS)).reshape(o_ref.shape)
    o_ref[...] = (yn * ln_ref[0:1, :] + ln_ref[1:2, :] + bonus_ref[...]) * g_ref[...]

    @pl.when(ci == nchunks - 1)
    def _():
        st_ref[...] = a_sc[...]


def _rw_scan(prep, ln, s0, *, nseq, t, c, valid, name):
    w = 512
    nchunks = -(-valid // c)
    npair = w // LANES
    nsb = RW_SEQS_PER_STEP if nseq % RW_SEQS_PER_STEP == 0 else 1
    blk = pl.BlockSpec((nsb, c, w), lambda b, i: (b, i, 0))
    st_spec = pl.BlockSpec((nsb, npair, LANES, LANES), lambda b, i: (b, 0, 0, 0))
    o, st = pl.pallas_call(
        functools.partial(_rw_scan_kernel, c=c, valid=valid, nchunks=nchunks),
        grid=(nseq // nsb, nchunks),
        in_specs=[blk] * 8 + [pl.BlockSpec((2, w), lambda b, i: (0, 0)), st_spec],
        out_specs=[blk, st_spec],
        out_shape=[
            jax.ShapeDtypeStruct((nseq, nchunks * c, w), F32),
            jax.ShapeDtypeStruct((nseq, npair, LANES, LANES), F32),
        ],
        scratch_shapes=[pltpu.VMEM((nsb, npair, LANES, LANES), F32)],
        compiler_params=_cparams("arbitrary", "arbitrary"),
        name=name,
    )(*[a.reshape(nseq, t, w) for a in prep], ln, s0)
    return o.reshape(nseq * nchunks * c, w), st


GL_PREC = 1


def _gla_kernel(*refs, tb, valid, nblocks, dk, nsb):
    seq_refs = [refs[5 * s : 5 * s + 5] for s in range(nsb)]
    ab_ref, abias_ref, ng_ref, s0_ref, o_ref, st_ref, s_sc = refs[5 * nsb :]
    bi = pl.program_id(1)

    @pl.when(bi == 0)
    def _():
        s_sc[...] = s0_ref[...]

    mm = functools.partial(_mm, prec=GL_PREC)
    ti = _iota2((tb, tb), 0)
    si = _iota2((tb, tb), 1)
    tri = _onehot(si <= ti)
    trow = _iota2((tb, 1), 0)
    lane_head = _iota2((1, LANES), 1) >> _log2(dk)
    heads_per_tile = LANES // dk
    own_lane = (_iota2((heads_per_tile * tb, 1), 0) >> _log2(tb)) == lane_head

    def dup(x):
        return jnp.concatenate([x] * heads_per_tile, axis=0)

    def stack(x):
        return jnp.where(own_lane, dup(x), 0.0)

    npair = seq_refs[0][0].shape[1] // LANES
    chains = [(s, p) for s in range(nsb) for p in range(npair)]
    q_l, k_l, la_l = [], [], []
    for s in range(nsb):
        q_ref, k_ref, _, xa_ref, _ = seq_refs[s]
        la_all = -_softplus(-(_mm(xa_ref[...], ab_ref[...], prec=3) + abias_ref[...])) * (1.0 / GL_TAU)
        k_all = k_ref[...]
        if valid < tb:
            live = (trow + bi * tb) < valid
            la_all = jnp.where(live, la_all, 0.0)
            k_all = jnp.where(live, k_all, 0.0)
        for p in range(npair):
            sl = slice(LANES * p, LANES * (p + 1))
            q_l.append(q_ref[:, sl] * (dk**-0.5))
            k_l.append(k_all[:, sl])
            la_l.append(la_all[:, sl])
    b_l = [_mm(tri, la, prec="b") for la in la_l]
    att_l = [jnp.where(dup(ti == si), mm(stack(q), k, NT), 0.0) for q, k in zip(q_l, k_l)]
    lev = 1
    while (1 << lev) <= tb:
        half = 1 << (lev - 1)
        sel = _onehot(si == ((ti >> lev) << lev) + (half - 1))
        upper = ((trow >> (lev - 1)) & 1) == 1
        same = dup((ti >> lev) == (si >> lev))
        bref_l = [_mm(sel, b, prec="b") for b in b_l]
        qs_l = [stack(jnp.where(upper, q * jnp.exp(jnp.minimum(b - br, 0.0)), 0.0)) for q, b, br in zip(q_l, b_l, bref_l)]
        ks_l = [jnp.where(upper, 0.0, k * jnp.exp(jnp.minimum(br - b, 0.0))) for k, b, br in zip(k_l, b_l, bref_l)]
        att_l = [att + jnp.where(same, mm(qs, ks, NT), 0.0) for att, qs, ks in zip(att_l, qs_l, ks_l)]
        lev += 1

    s_old_l = [s_sc[s, p] for s, p in chains]
    inter_l = [mm(stack(q * jnp.exp(b)), s_old, NT) for q, b, s_old in zip(q_l, b_l, s_old_l)]
    outs, states = [], []
    for i, (s, p) in enumerate(chains):
        b_end = b_l[i][tb - 1 : tb, :]
        ke = k_l[i] * jnp.exp(b_end - b_l[i])
        s_new = s_old_l[i] * jnp.exp(b_end)
        for h in range(heads_per_tile):
            hv = heads_per_tile * p + h
            v = seq_refs[s][2][:, LANES * hv : LANES * (hv + 1)]
            o = mm(att_l[i][h * tb : (h + 1) * tb], v) + inter_l[i][h * tb : (h + 1) * tb]
            s_new = s_new + mm(v, jnp.where(lane_head == h, ke, 0.0), TN)
            outs.append(o * lax.rsqrt(jnp.mean(o * o, axis=-1, keepdims=True) + NORM_EPS) * ng_ref[...])
        states.append(s_new)

    s_sc[...] = jnp.stack(states).reshape(s_sc.shape)
    per_seq = len(outs) // nsb
    for s in range(nsb):
        gg = seq_refs[s][4][...]
        o_ref[s] = jnp.concatenate(outs[s * per_seq : (s + 1) * per_seq], axis=1) * (gg * _sigmoid(gg))

    @pl.when(bi == nblocks - 1)
    def _():
        st_ref[...] = s_sc[...]


def _gla(proj, cols, ab, abias, ng, s0, *, nseq, t, tb, valid, name):
    cq, ck, cv, cxa, cgg = cols
    nb = t // tb
    nrun = -(-valid // tb)
    npair = 256 // LANES
    nsb = GL_SEQS_PER_STEP if nseq % GL_SEQS_PER_STEP == 0 else 1
    st_spec = pl.BlockSpec((nsb, npair, LANES, LANES), lambda b, i: (b, 0, 0, 0))

    def seq_specs(s):
        row = lambda b, i: (b * nsb + s) * nb + i
        return [
            pl.BlockSpec((tb, 256), lambda b, i: (row(b, i), cq)),
            pl.BlockSpec((tb, 256), lambda b, i: (row(b, i), ck)),
            pl.BlockSpec((tb, 512), lambda b, i: (row(b, i), cv)),
            pl.BlockSpec((tb, LANES), lambda b, i: (row(b, i), cxa)),
            pl.BlockSpec((tb, 512), lambda b, i: (row(b, i), cgg)),
        ]

    o, st = pl.pallas_call(
        functools.partial(_gla_kernel, tb=tb, valid=valid, nblocks=nrun, dk=256 // GL_HEADS, nsb=nsb),
        grid=(nseq // nsb, nrun),
        in_specs=[spec for s in range(nsb) for spec in seq_specs(s)]
        + [
            pl.BlockSpec((LANES, 256), lambda b, i: (0, 0)),
            pl.BlockSpec((1, 256), lambda b, i: (0, 0)),
            pl.BlockSpec((1, LANES), lambda b, i: (0, 0)),
            st_spec,
        ],
        out_specs=[pl.BlockSpec((nsb, tb, 512), lambda b, i: (b, i, 0)), st_spec],
        out_shape=[
            jax.ShapeDtypeStruct((nseq, nrun * tb, 512), F32),
            jax.ShapeDtypeStruct((nseq, npair, LANES, LANES), F32),
        ],
        scratch_shapes=[pltpu.VMEM((nsb, npair, LANES, LANES), F32)],
        compiler_params=_cparams("arbitrary", "arbitrary"),
        name=name,
    )(*([proj] * (5 * nsb)), ab, abias, ng, s0)
    return o.reshape(nseq * nrun * tb, 512), st


def _fx_prep_kernel(*refs, stacked):
    q_ref, k_ref, v_ref, fl_ref, qg_ref, kg_ref, bf_ref = refs[:7]
    qb_o, kf_o, vf_o, kb_o, vb_o, lf_o, c_o, carry = refs[9:] if stacked else refs[7:]
    tb = q_ref.shape[0]

    @pl.when(pl.program_id(1) == 0)
    def _():
        carry[...] = jnp.zeros_like(carry)

    def headnorm(x, gain):
        return x * lax.rsqrt(jnp.mean(x * x, axis=-1, keepdims=True) + NORM_EPS) * gain

    nh = q_ref.shape[1] // FX_HD
    for h in range(nh):
        sl = slice(FX_HD * h, FX_HD * (h + 1))
        qb_o[:, sl] = (headnorm(q_ref[:, sl], qg_ref[...]) * (FX_HD**-0.5 * LOG2E)).astype(BF16)
        kn = headnorm(k_ref[:, sl], kg_ref[...])
        kb_o[:, sl] = kn.astype(BF16)
        kf_o[pl.ds(h, tb, stride=nh), :] = kn
        vf_o[pl.ds(h, tb, stride=nh), :] = v_ref[:, sl]
    vb_o[...] = v_ref[...].astype(BF16)
    logf = -_softplus(-(fl_ref[...] + bf_ref[...]))
    lf_o[...] = logf
    tri = _onehot(_iota2((tb, tb), 1) <= _iota2((tb, tb), 0))
    c = _mm(tri, logf, prec="b") + carry[...]
    c_o[...] = c
    carry[...] = c[tb - 1 : tb, :]


def _fx_prep(proj, cols, qg, kg, bf, *, nseq, t, tb, name, stack=None):
    cq, ck, cv, cfl = cols
    nt = t // tb
    row = lambda b, i: b * nt + i
    w = 1024
    nh = w // FX_HD
    wide = pl.BlockSpec((tb, w), lambda b, i: (row(b, i), 0))
    thin = pl.BlockSpec((tb, LANES), lambda b, i: (row(b, i), 0))
    vecspec = pl.BlockSpec((1, LANES), lambda b, i: (0, 0))
    rows = nseq * t
    extra_in, extra_args, aliases = [], [], {}
    kv_spec = pl.BlockSpec((tb * nh, FX_HD), lambda b, i: (row(b, i), 0))
    kv_shape = jax.ShapeDtypeStruct((rows * nh, FX_HD), F32)
    if stack is not None:
        k_all, v_all, layer = stack
        extra_in = [pl.BlockSpec(memory_space=pl.ANY)] * 2
        extra_args = [k_all, v_all]
        aliases = {7: 1, 8: 2}
        kv_spec = pl.BlockSpec((None, None, tb * nh, FX_HD), lambda b, i: (b, layer, i, 0))
        kv_shape = jax.ShapeDtypeStruct(k_all.shape, F32)
    return pl.pallas_call(
        functools.partial(_fx_prep_kernel, stacked=stack is not None),
        grid=(nseq, nt),
        in_specs=[
            pl.BlockSpec((tb, w), lambda b, i: (row(b, i), cq)),
            pl.BlockSpec((tb, w), lambda b, i: (row(b, i), ck)),
            pl.BlockSpec((tb, w), lambda b, i: (row(b, i), cv)),
            pl.BlockSpec((tb, LANES), lambda b, i: (row(b, i), cfl)),
            vecspec, vecspec, vecspec,
        ] + extra_in,
        out_specs=[wide, kv_spec, kv_spec, wide, wide, thin, thin],
        out_shape=[
            jax.ShapeDtypeStruct((rows, w), BF16),
            kv_shape,
            kv_shape,
            jax.ShapeDtypeStruct((rows, w), BF16),
            jax.ShapeDtypeStruct((rows, w), BF16),
            jax.ShapeDtypeStruct((rows, LANES), F32),
            jax.ShapeDtypeStruct((rows, LANES), F32),
        ],
        scratch_shapes=[pltpu.VMEM((1, LANES), F32)],
        input_output_aliases=aliases,
        compiler_params=_cparams("arbitrary", "arbitrary"),
        name=name,
    )(proj, proj, proj, proj, qg, kg, bf, *extra_args)


def _with_ones(v):
    return jnp.concatenate([v, jnp.ones_like(v)], axis=1)


def _flash_kernel(qi_ref, kj_ref, q_ref, k_ref, v_ref, c_ref, cref_ref, o_ref, m_sc, acc_sc, *, tq, tk, hb):
    qi = qi_ref[pl.program_id(2)]
    kj = kj_ref[pl.program_id(2)]

    @pl.when(kj == 0)
    def _():
        m_sc[...] = jnp.full_like(m_sc, -jnp.inf)
        acc_sc[...] = jnp.zeros_like(acc_sc)

    def update(masked):
        heads = range(hb)
        sl = [slice(FX_HD * h, FX_HD * (h + 1)) for h in heads]
        s_l = [lax.dot_general(q_ref[:, sl[h]], k_ref[:, sl[h]], NT, preferred_element_type=F32) for h in heads]
        s_l = [s_l[h] + (cref_ref[0, h, :, 0:1] - c_ref[0, h]) * LOG2E for h in heads]
        if masked:
            keep = (kj * tk + _iota2((tq, tk), 1)) <= (qi * tq + _iota2((tq, tk), 0))
            s_l = [jnp.where(keep, s, -jnp.inf) for s in s_l]
        m_old = [m_sc[h] for h in heads]
        m_new = [jnp.maximum(m_old[h], jnp.max(s_l[h], axis=-1, keepdims=True)) for h in heads]
        p_l = [jnp.exp2(s_l[h] - pltpu.repeat(m_new[h], tk // LANES, axis=1)) for h in heads]
        pv_l = [jnp.dot(p_l[h].astype(BF16), _with_ones(v_ref[:, sl[h]]), preferred_element_type=F32) for h in heads]
        for h in heads:
            alpha = jnp.exp2(m_old[h] - m_new[h])
            acc_sc[h] = pltpu.repeat(alpha, 2, axis=1) * acc_sc[h] + pv_l[h]
            m_sc[h] = m_new[h]

    first_key, last_key = kj * tk, kj * tk + (tk - 1)
    first_query, last_query = qi * tq, qi * tq + (tq - 1)

    @pl.when(last_key <= first_query)
    def _():
        update(False)

    @pl.when(last_key > first_query)
    def _():
        update(True)

    @pl.when(first_key + tk > last_query)
    def _():
        for h in range(hb):
            acc = acc_sc[h]
            o_ref[:, FX_HD * h : FX_HD * (h + 1)] = acc[:, :FX_HD] / acc[:, FX_HD:]


def _flash(qb, kb, vb, c_row, *, nseq, t, tq, tk, name):
    nh = qb.shape[1] // FX_HD
    hb = FLASH_HEADS_PER_STEP if nh % FLASH_HEADS_PER_STEP == 0 else 1
    nq, nk = t // tq, t // tk
    pairs = [(i, j) for i in range(nq) for j in range(nk) if j * tk <= i * tq + tq - 1]
    qi_tab = jnp.array([i for i, _ in pairs], jnp.int32)
    kj_tab = jnp.array([j for _, j in pairs], jnp.int32)
    w = hb * FX_HD
    grid_spec = pltpu.PrefetchScalarGridSpec(
        num_scalar_prefetch=2,
        grid=(nseq, nh // hb, len(pairs)),
        in_specs=[
            pl.BlockSpec((tq, w), lambda b, h, p, qt, kt: (b * nq + qt[p], h)),
            pl.BlockSpec((tk, w), lambda b, h, p, qt, kt: (b * nk + kt[p], h)),
            pl.BlockSpec((tk, w), lambda b, h, p, qt, kt: (b * nk + kt[p], h)),
            pl.BlockSpec((1, hb, 1, tk), lambda b, h, p, qt, kt: (b, h, 0, kt[p])),
            pl.BlockSpec((1, hb, 1, LANES), lambda b, h, p, qt, kt: (b, h, 0, qt[p] * (tq // LANES))),
        ],
        out_specs=pl.BlockSpec((tq, w), lambda b, h, p, qt, kt: (b * nq + qt[p], h)),
        scratch_shapes=[pltpu.VMEM((hb, tq, LANES), F32), pltpu.VMEM((hb, tq, 2 * FX_HD), F32)],
    )
    return pl.pallas_call(
        functools.partial(_flash_kernel, tq=tq, tk=tk, hb=hb),
        grid_spec=grid_spec,
        out_shape=jax.ShapeDtypeStruct((nseq * t, nh * FX_HD), F32),
        compiler_params=_cparams("arbitrary", "arbitrary", "arbitrary"),
        name=name,
    )(qi_tab, kj_tab, qb, kb, vb, c_row, c_row)


def _decode_kernel(pt_ref, q_ref, kn_ref, vn_ref, cn_ref, *rest, nq, nh, pp):
    kc_refs, vc_refs, lf_refs = rest[0:pp], rest[pp : 2 * pp], rest[2 * pp : 3 * pp]
    o_ref, m_sc, acc_sc, carry = rest[3 * pp :]
    j = pl.program_id(1)
    heads = range(nh)
    q_l = [q_ref[:, FX_HD * h : FX_HD * (h + 1)] for h in heads]

    def attend(pages, causal):
        kv = [[load(h) for h in heads] for load, _ in pages]
        s = [[lax.dot_general(q_l[h], kv[i][h][0], NT, preferred_element_type=F32) + pages[i][1][h : h + 1, :]
              for h in heads] for i in range(len(pages))]
        if causal:
            keep = _iota2((nq, PAGE), 1) <= _iota2((nq, PAGE), 0)
            s = [[jnp.where(keep, x, -jnp.inf) for x in row] for row in s]
        for h in heads:
            m_old = m_sc[h]
            m_new = m_old
            for i in range(len(pages)):
                m_new = jnp.maximum(m_new, jnp.max(s[i][h], axis=-1, keepdims=True))
            acc = pltpu.repeat(jnp.exp2(m_old - m_new), 2, axis=1) * acc_sc[h]
            for i in range(len(pages)):
                p = jnp.exp2(s[i][h] - m_new)
                acc = acc + jnp.dot(p.astype(BF16), _with_ones(kv[i][h][1]), preferred_element_type=F32)
            acc_sc[h] = acc
            m_sc[h] = m_new

    @pl.when(j == 0)
    def _():
        m_sc[...] = jnp.full_like(m_sc, -jnp.inf)
        acc_sc[...] = jnp.zeros_like(acc_sc)
        carry[...] = jnp.zeros_like(carry)

        def new_rows(h):
            sl = slice(FX_HD * h, FX_HD * (h + 1))
            return kn_ref[:, sl], vn_ref[:, sl]

        attend([(new_rows, -cn_ref[0] * LOG2E)], True)

    @pl.when(j > 0)
    def _():
        after = _onehot(_iota2((PAGE, PAGE), 0) > _iota2((PAGE, PAGE), 1))
        both = jnp.concatenate([after, jnp.ones((PAGE, PAGE), BF16)], axis=1)
        run = carry[...]
        pages = []
        for i in range(pp):

            def cached_rows(h, i=i):
                rows = pl.ds(h, PAGE, stride=nh)
                return kc_refs[i][rows, :].astype(BF16), vc_refs[i][rows, :].astype(BF16)

            sums = _mm(lf_refs[i][...], both, prec="a")
            pages.append((cached_rows, (run + sums[:, :PAGE]) * LOG2E))
            run = run + sums[:, PAGE:]
        carry[...] = run
        attend(pages, False)

    @pl.when(j == pl.num_programs(1) - 1)
    def _():
        for h in heads:
            acc = acc_sc[h]
            o_ref[:, FX_HD * h : FX_HD * (h + 1)] = acc[:, :FX_HD] / acc[:, FX_HD:]


def _decode(qb, kb, vb, cn_t, cache_k, cache_v, cache_logf, page_table, layer, *, nq, name):
    nseq, npages = page_table.shape
    pool, depth, _, nh, hd = cache_k.shape
    w = nh * hd
    pp = DEC_PAGES_PER_STEP if npages % DEC_PAGES_PER_STEP == 0 else 1
    ck = cache_k.reshape(pool, depth, PAGE * nh, hd)
    cv = cache_v.reshape(pool, depth, PAGE * nh, hd)
    lf_t = jnp.swapaxes(cache_logf, 2, 3)

    def page_of(i):
        def index(b, j, pt):
            back = (jnp.maximum(j, 1) - 1) * pp + i
            return (pt[b * npages + (npages - 1 - back)], layer, 0, 0)

        return index

    grid_spec = pltpu.PrefetchScalarGridSpec(
        num_scalar_prefetch=1,
        grid=(nseq, 1 + npages // pp),
        in_specs=[
            pl.BlockSpec((nq, w), lambda b, j, pt: (b * (PAGE // nq), 0)),
            pl.BlockSpec((PAGE, w), lambda b, j, pt: (b, 0)),
            pl.BlockSpec((PAGE, w), lambda b, j, pt: (b, 0)),
            pl.BlockSpec((1, nh, PAGE), lambda b, j, pt: (b, 0, 0)),
        ]
        + [pl.BlockSpec((None, None, PAGE * nh, hd), page_of(i)) for i in range(pp)] * 2
        + [pl.BlockSpec((None, None, nh, PAGE), page_of(i)) for i in range(pp)],
        out_specs=pl.BlockSpec((nq, w), lambda b, j, pt: (b, 0)),
        scratch_shapes=[
            pltpu.VMEM((nh, nq, LANES), F32),
            pltpu.VMEM((nh, nq, 2 * hd), F32),
            pltpu.VMEM((nh, PAGE), F32),
        ],
    )
    return pl.pallas_call(
        functools.partial(_decode_kernel, nq=nq, nh=nh, pp=pp),
        grid_spec=grid_spec,
        out_shape=jax.ShapeDtypeStruct((nseq * nq, w), F32),
        compiler_params=_cparams("arbitrary", "arbitrary"),
        name=name,
    )(page_table.reshape(-1), qb, kb, vb, cn_t, *([ck] * pp), *([cv] * pp), *([lf_t] * pp))


RW_CHUNK = 64
RW_SEQS_PER_STEP = 2
GL_BLOCK = 128
GL_SEQS_PER_STEP = 2
FLASH_HEADS_PER_STEP = 4
DEC_Q_ROWS = 16
DEC_PAGES_PER_STEP = 8
TM = 768
COL_FX_Q, COL_FX_K, COL_FX_V = 0, 1024, 2048
COL_RW_R = 3072
COL_GL_V, COL_GL_GG = 4608, 5120
COL_GL_Q, COL_GL_K = 5632, 5888
COL_RW_Z = 6144
COL_FX_FL = 6400
COL_GL_XA = 6528
COLS_PAD = 6656


def _pad_cols(x, width):
    return jnp.pad(x, [(0, 0)] * (x.ndim - 1) + [(0, width - x.shape[-1])])


def _relayout_in(w, rw_w, fx_w, gl_kw, gl_vw, nh_fx):
    rw_z = RW_DECAY_LORA + RW_A_LORA + RW_G_LORA
    o = 0
    rw_rkv = w[..., o : o + 3 * rw_w]; o += 3 * rw_w
    rw_zz = w[..., o : o + rw_z]; o += rw_z
    fx_qkv = w[..., o : o + 3 * fx_w]; o += 3 * fx_w
    fx_fl = w[..., o : o + nh_fx]; o += nh_fx
    gl_qk = w[..., o : o + 2 * gl_kw]; o += 2 * gl_kw
    gl_v = w[..., o : o + gl_vw]; o += gl_vw
    gl_xa = w[..., o : o + GL_GATE_LORA]; o += GL_GATE_LORA
    gl_gg = w[..., o : o + gl_vw]; o += gl_vw
    assert o == w.shape[-1]
    return jnp.concatenate(
        [fx_qkv, rw_rkv, gl_v, gl_gg, gl_qk, _pad_cols(rw_zz, 256), _pad_cols(fx_fl, LANES), _pad_cols(gl_xa, LANES)],
        axis=-1,
    )


def _mixers(proj, layer, lp, states, paged, *, nseq, t, valid, tag, kv_stack=None):
    (shift0, rw_s0, gl_s0) = states
    tb = min(t, 256)
    prep = _rw_prep(proj, COL_RW_R // 512, COL_RW_Z // 256, shift0, lp["rw_vec"], lp["rw_muz"], lp["rw_lora"],
                    nseq=nseq, t=t, tb=tb, name=f"rw_prep_{tag}")
    o_rw, rw_state = _rw_scan(prep, lp["rw_ln"], rw_s0, nseq=nseq, t=t, c=min(t, RW_CHUNK), valid=valid,
                              name=f"rw_scan_{tag}")
    o_gl, gl_state = _gla(
        proj, (COL_GL_Q // 256, COL_GL_K // 256, COL_GL_V // 512, COL_GL_XA // LANES, COL_GL_GG // 512),
        lp["gl_ab"], lp["gl_abias"], lp["gl_ng"], gl_s0, nseq=nseq, t=t, tb=min(t, GL_BLOCK), valid=valid,
        name=f"gla_{tag}")
    qb, kf, vf, kb, vb, logf, c = _fx_prep(
        proj, (COL_FX_Q // 1024, COL_FX_K // 1024, COL_FX_V // 1024, COL_FX_FL // LANES),
        lp["fx_qg"], lp["fx_kg"], lp["fx_bf"], nseq=nseq, t=t, tb=tb, name=f"fx_prep_{tag}",
        stack=None if kv_stack is None else (*kv_stack, layer))
    nh = kb.shape[1] // FX_HD
    c_t = jnp.transpose(c.reshape(nseq, t, LANES)[:, :, :nh], (0, 2, 1))
    if paged is None:
        blk = min(t, 512)
        o_fx = _flash(qb, kb, vb, c_t[:, :, None, :], nseq=nseq, t=t, tq=blk, tk=blk, name=f"flash_{tag}")
    else:
        cache_k, cache_v, cache_logf, page_table = paged
        nq = DEC_Q_ROWS
        assert t == PAGE and valid <= nq
        o_fx = _decode(qb, kb, vb, c_t, cache_k, cache_v, cache_logf, page_table, layer, nq=nq, name=f"decode_{tag}")
        o_fx = o_fx.reshape(nseq, nq, -1)[:, :valid].reshape(nseq * valid, -1)
    return o_rw, o_fx, o_gl, kf, logf, rw_state, gl_state, vf


def kernel(x_prompt, x_sample, cache_k, cache_v, cache_logf, state_rwkv, state_rwkv_shift, state_gla,
           page_table, attn_norm_g, w_in, rw_mu, rw_w_b, rw_w0, rw_a_b, rw_a0, rw_g_b, rw_k_k, rw_k_a,
           rw_r_k, rw_lnx_g, rw_lnx_b, fx_qn_g, fx_kn_g, fx_b_f, gl_alpha_b, gl_alpha_bias, gl_norm_g,
           w_out, ffn_norm_g, w_gu, w_down):
    nb, seq, d_model = x_prompt.shape
    ns, dseq, _ = x_sample.shape
    depth = w_in.shape[0]
    rw_w = rw_w0.shape[1]
    rw_heads = rw_w // RW_HD
    nh_fx = fx_b_f.shape[1]
    fx_w = nh_fx * FX_HD
    gl_kw = gl_alpha_b.shape[2]
    gl_dk = gl_kw // GL_HEADS
    gl_dv = gl_norm_g.shape[1]
    gl_vw = GL_HEADS * gl_dv
    rw_cols = 3 * rw_w + RW_DECAY_LORA + RW_A_LORA + RW_G_LORA
    assert (rw_w, fx_w, gl_kw, gl_vw, FX_HD, gl_dv) == (512, 1024, 256, 512, 128, 128)
    n_prompt = nb * seq
    n_sample = ns * dseq
    rows = n_prompt + n_sample
    tm = TM if rows >= TM else 64
    rows_pad = -(-rows // tm) * tm
    relayout = functools.partial(_relayout_in, rw_w=rw_w, fx_w=fx_w, gl_kw=gl_kw, gl_vw=gl_vw, nh_fx=nh_fx)

    w_in_p = relayout(w_in).astype(BF16)
    w_out_b, w_gu_b, w_down_b = w_out.astype(BF16), w_gu.astype(BF16), w_down.astype(BF16)
    rw_mu_r, rw_mu_k, rw_mu_v, rw_mu_z = (
        rw_mu[:, :rw_w], rw_mu[:, rw_w : 2 * rw_w], rw_mu[:, 2 * rw_w : 3 * rw_w], rw_mu[:, 3 * rw_w :])
    layers = []
    for l in range(depth):
        layers.append(dict(
            rw_vec=jnp.stack([rw_w0[l], rw_a0[l], rw_k_k[l], rw_k_a[l], rw_r_k[l].reshape(-1),
                              rw_mu_r[l], rw_mu_k[l], rw_mu_v[l]]),
            rw_muz=_pad_cols(rw_mu_z[l][None], 256),
            rw_lora=jnp.pad(jnp.concatenate([rw_w_b[l], rw_a_b[l], rw_g_b[l]], axis=0),
                            ((0, 256 - (rw_cols - 3 * rw_w)), (0, 0))),
            rw_ln=jnp.stack([rw_lnx_g[l], rw_lnx_b[l]]),
            fx_qg=fx_qn_g[l][None], fx_kg=fx_kn_g[l][None], fx_bf=_pad_cols(fx_b_f[l][None], LANES),
            gl_ab=jnp.pad(gl_alpha_b[l], ((0, LANES - GL_GATE_LORA), (0, 0))),
            gl_abias=gl_alpha_bias[l][None], gl_ng=gl_norm_g[l][None],
        ))

    def rw_state_in(s):
        n = s.shape[0]
        a = jnp.swapaxes(s, -1, -2).reshape(n, rw_heads // 2, 2, RW_HD, RW_HD)
        z = jnp.zeros_like(a[:, :, 0])
        top = jnp.concatenate([a[:, :, 0], z], axis=-1)
        bot = jnp.concatenate([z, a[:, :, 1]], axis=-1)
        return jnp.concatenate([top, bot], axis=-2)

    def rw_state_out(a):
        n = a.shape[0]
        h0 = a[:, :, :RW_HD, :RW_HD]
        h1 = a[:, :, RW_HD:, RW_HD:]
        return jnp.swapaxes(jnp.stack([h0, h1], axis=2).reshape(n, rw_heads, RW_HD, RW_HD), -1, -2)

    def gl_state_in(s):
        n = s.shape[0]
        a = jnp.swapaxes(s, -1, -2).reshape(n, GL_HEADS // 2, 2, gl_dv, gl_dk)
        return jnp.transpose(a, (0, 1, 3, 2, 4)).reshape(n, GL_HEADS // 2, gl_dv, 2 * gl_dk)

    def gl_state_out(a):
        n = a.shape[0]
        a = jnp.transpose(a.reshape(n, GL_HEADS // 2, gl_dv, 2, gl_dk), (0, 1, 3, 2, 4))
        return jnp.swapaxes(a.reshape(n, GL_HEADS, gl_dv, gl_dk), -1, -2)

    def shift_in(s):
        return (s[:, None, :rw_w], s[:, None, rw_w : 2 * rw_w], s[:, None, 2 * rw_w : 3 * rw_w],
                _pad_cols(s[:, None, 3 * rw_w :], 256))

    def shift_out(proj_rows):
        return jnp.concatenate([proj_rows[:, COL_RW_R : COL_RW_R + 3 * rw_w],
                                proj_rows[:, COL_RW_Z : COL_RW_Z + rw_cols - 3 * rw_w]], axis=-1)

    x = jnp.concatenate([x_prompt.reshape(n_prompt, d_model), x_sample.reshape(n_sample, d_model),
                         jnp.zeros((rows_pad - rows, d_model), x_prompt.dtype)], axis=0)
    t_s = PAGE
    p_states = (shift_in(jnp.zeros((nb, rw_cols), F32)), jnp.zeros((nb, rw_heads // 2, LANES, LANES), F32),
                jnp.zeros((nb, GL_HEADS // 2, LANES, LANES), F32))
    outs = {k: [] for k in ("plf", "prw", "psh", "pgl", "sk", "sv", "slf", "srw", "ssh", "sgl")}
    pk_all = jnp.zeros((nb, depth, seq * nh_fx, FX_HD), F32)
    pv_all = jnp.zeros((nb, depth, seq * nh_fx, FX_HD), F32)
    for l in range(depth):
        lp = layers[l]
        proj = _dense(x, w_in_p, l, tm=tm, tn=512, g=attn_norm_g[l], name=f"w_in_{l}")
        proj_s = jnp.pad(proj[n_prompt:rows].reshape(ns, dseq, COLS_PAD), ((0, 0), (0, t_s - dseq), (0, 0)))
        proj_s = proj_s.reshape(ns * t_s, COLS_PAD)
        s_states = (shift_in(state_rwkv_shift[:, l]), rw_state_in(state_rwkv[:, l]), gl_state_in(state_gla[:, l]))
        po = _mixers(proj, l, lp, p_states, None, nseq=nb, t=seq, valid=seq, tag=f"p{l}", kv_stack=(pk_all, pv_all))
        pk_all, pv_all = po[3], po[7]
        so = _mixers(proj_s, l, lp, s_states, (cache_k, cache_v, cache_logf, page_table),
                     nseq=ns, t=t_s, valid=dseq, tag=f"s{l}")

        def unpad(a):
            return a.reshape(ns, a.shape[0] // ns, -1)[:, :dseq].reshape(n_sample, -1)

        mix = jnp.concatenate([
            jnp.concatenate([po[0], po[1], po[2]], axis=1),
            jnp.concatenate([unpad(so[0]), so[1], unpad(so[2])], axis=1),
            jnp.zeros((rows_pad - rows, d_model), F32)], axis=0)
        x = _dense(mix, w_out_b, l, tm=tm, tn=512, residual=x, name=f"w_out_{l}")
        act = _dense(x, w_gu_b, l, tm=tm, tn=512, g=ffn_norm_g[l], swiglu=True, out_dtype=BF16, name=f"w_gu_{l}")
        x = _dense(act, w_down_b, l, tm=tm, tn=256, residual=x, name=f"w_down_{l}")

        outs["plf"].append(po[4][:, :nh_fx].reshape(nb, seq, nh_fx))
        outs["prw"].append(rw_state_out(po[5]))
        outs["psh"].append(shift_out(jnp.stack([proj[b * seq + seq - 1] for b in range(nb)])))
        outs["pgl"].append(gl_state_out(po[6]))
        outs["sk"].append(so[3].reshape(ns, t_s, nh_fx, FX_HD)[:, :dseq])
        outs["sv"].append(so[7].reshape(ns, t_s, nh_fx, FX_HD)[:, :dseq])
        outs["slf"].append(unpad(so[4])[:, :nh_fx].reshape(ns, dseq, nh_fx))
        outs["srw"].append(rw_state_out(so[5]))
        outs["ssh"].append(shift_out(proj[n_prompt:rows].reshape(ns, dseq, COLS_PAD)[:, dseq - 1]))
        outs["sgl"].append(gl_state_out(so[6]))

    st = {k: jnp.stack(v, axis=1) for k, v in outs.items()}
    y_prompt = x[:n_prompt].reshape(nb, seq, d_model)
    y_sample = x[n_prompt:rows].reshape(ns, dseq, d_model)
    kv_shape = (nb, depth, seq, nh_fx, FX_HD)
    return (y_prompt, y_sample, pk_all.reshape(kv_shape), pv_all.reshape(kv_shape), st["plf"], st["prw"], st["psh"], st["pgl"],
            st["sk"], st["sv"], st["slf"], st["srw"], st["ssh"], st["sgl"])
```
